```python
import jax, jax.numpy as jnp
from jax import lax
import numpy as np

D_MODEL = 1024
BATCH = 8
SEQ = 8192
DEPTH = 4

POOL_WINDOWS = (2, 4, 8, 16)
N_POOL_GROUPS = 4
POOL_GROUP_DIM = D_MODEL // 16
POOL_WIDTH = N_POOL_GROUPS * POOL_GROUP_DIM
HEAD_DIM = 64
N_Q_HEADS = D_MODEL // 128
N_KV_HEADS = 2
GROUP = N_Q_HEADS // N_KV_HEADS
Q_WIDTH = N_Q_HEADS * HEAD_DIM
KV_WIDTH = N_KV_HEADS * HEAD_DIM
WINDOW = 128
ROT_DIM = HEAD_DIM // 4
ROPE_THETA = 500000.0
CONV_WIDTH = D_MODEL // 4
CONV_K = 3
N_BRANCHES = 3
IN_WIDTH = POOL_WIDTH + Q_WIDTH + 2 * KV_WIDTH + 3 * CONV_WIDTH + N_BRANCHES * D_MODEL
D_FF = ((8 * D_MODEL // 3 + 127) // 128) * 128
FFN_K = 3
ALPHA = (2 * DEPTH) ** 0.25
BETA = (8 * DEPTH) ** -0.25
LN_EPS = 1e-5
MASK_VALUE = -1e30

kernel_name = "hybrid_pool_swa_shortconv_deepnorm"


def _split_points(sizes):
    pts, acc = [], 0
    for s in sizes[:-1]:
        acc += s
        pts.append(acc)
    return pts


def layer_norm(x, g, b):
    xf = x.astype(jnp.float32)
    mu = jnp.mean(xf, axis=-1, keepdims=True)
    var = jnp.mean(jnp.square(xf - mu), axis=-1, keepdims=True)
    y = (xf - mu) * lax.rsqrt(var + LN_EPS)
    return (y * g.astype(jnp.float32) + b.astype(jnp.float32)).astype(x.dtype)


def causal_dwconv(u, w):
    K = w.shape[0]
    S = u.shape[1]
    up = jnp.pad(u, ((0, 0), (K - 1, 0), (0, 0)))
    y = up[:, 0:S] * w[0]
    for k in range(1, K):
        y = y + up[:, k:k + S] * w[k]
    return y


def rope_tables(positions):
    inv_freq = ROPE_THETA ** (-jnp.arange(0, ROT_DIM, 2, dtype=jnp.float32) / ROT_DIM)
    ang = positions.astype(jnp.float32)[..., None] * inv_freq
    return jnp.cos(ang), jnp.sin(ang)


def apply_partial_rope(t, cos, sin):
    tf = t.astype(jnp.float32)
    half = ROT_DIM // 2
    x1, x2, rest = tf[..., :half], tf[..., half:ROT_DIM], tf[..., ROT_DIM:]
    c, s = cos[:, :, None, :], sin[:, :, None, :]
    out = jnp.concatenate([x1 * c - x2 * s, x2 * c + x1 * s, rest], axis=-1)
    return out.astype(t.dtype)


def multiscale_pool_mixer(u, w_pool, pool_scale):
    B, S, _ = u.shape
    ug = u.reshape(B, S, N_POOL_GROUPS, POOL_GROUP_DIM).astype(jnp.float32)
    csum = jnp.cumsum(ug, axis=1)
    t = jnp.arange(S)
    outs = []
    for g, w in enumerate(POOL_WINDOWS):
        c = csum[:, :, g]
        c_lag = jnp.pad(c[:, :S - w], ((0, 0), (w, 0), (0, 0)))
        count = jnp.minimum(t + 1, w).astype(jnp.float32)[None, :, None]
        outs.append((c - c_lag) / count - ug[:, :, g])
    pooled = jnp.stack(outs, axis=2).astype(u.dtype)
    mixed = jnp.einsum('bsgc,gcd->bsgd', pooled, w_pool)
    return mixed.reshape(B, S, POOL_WIDTH) * pool_scale


def sliding_window_gqa_sinks(q, k, v, sinks):
    B, S, _, D = q.shape
    nb = S // WINDOW
    qb = q.reshape(B, nb, WINDOW, N_KV_HEADS, GROUP, D)
    kb = k.reshape(B, nb, WINDOW, N_KV_HEADS, D)
    vb = v.reshape(B, nb, WINDOW, N_KV_HEADS, D)
    pad = ((0, 0), (1, 0), (0, 0), (0, 0), (0, 0))
    kcat = jnp.concatenate([jnp.pad(kb[:, :-1], pad), kb], axis=2)
    vcat = jnp.concatenate([jnp.pad(vb[:, :-1], pad), vb], axis=2)
    scores = jnp.einsum('bnqhgd,bnkhd->bnhgqk', qb, kcat).astype(jnp.float32)
    scores = scores * (HEAD_DIM ** -0.5)
    i = jnp.arange(WINDOW)[:, None]
    j = jnp.arange(2 * WINDOW)[None, :]
    band = (j > i) & (j <= i + WINDOW)
    blk = jnp.arange(nb)[:, None, None]
    valid = band[None] & ((blk > 0) | (j[None] >= WINDOW))
    scores = jnp.where(valid[None, :, None, None], scores, MASK_VALUE)
    sink = jnp.broadcast_to(
        sinks.astype(jnp.float32).reshape(1, 1, N_KV_HEADS, GROUP, 1, 1),
        scores.shape[:-1] + (1,))
    probs = jax.nn.softmax(jnp.concatenate([scores, sink], axis=-1), axis=-1)[..., :-1]
    out = jnp.einsum('bnhgqk,bnkhd->bnqhgd', probs.astype(v.dtype), vcat)
    return out.reshape(B, S, N_Q_HEADS * D)


def short_gated_conv(xc, gate_b, gate_c, conv_w):
    return gate_b * causal_dwconv(gate_c * xc, conv_w)


def hybrid_layer(x, cos, sin, w_in, w_pool, pool_scale, attn_sinks, conv_w,
                 w_branch_a, w_branch_b, w_branch_c, w_o, ln1_g, ln1_b,
                 w_up, ffn_conv_w, w_down, ln2_g, ln2_b):
    B, S, _ = x.shape
    proj = jnp.einsum('bsd,de->bse', x, w_in)
    sizes = [POOL_WIDTH, Q_WIDTH, KV_WIDTH, KV_WIDTH,
             CONV_WIDTH, CONV_WIDTH, CONV_WIDTH, N_BRANCHES * D_MODEL]
    u_pool, q, k, v, xc, gate_b, gate_c, gate_logits = jnp.split(
        proj, _split_points(sizes), axis=-1)
    o_a = multiscale_pool_mixer(u_pool, w_pool, pool_scale)
    q = apply_partial_rope(q.reshape(B, S, N_Q_HEADS, HEAD_DIM), cos, sin)
    k = apply_partial_rope(k.reshape(B, S, N_KV_HEADS, HEAD_DIM), cos, sin)
    v = v.reshape(B, S, N_KV_HEADS, HEAD_DIM)
    o_b = sliding_window_gqa_sinks(q, k, v, attn_sinks)
    o_c = short_gated_conv(xc, gate_b, gate_c, conv_w)
    gates = jax.nn.sigmoid(gate_logits).reshape(B, S, N_BRANCHES, D_MODEL)
    merged = (gates[:, :, 0] * jnp.einsum('bsc,cd->bsd', o_a, w_branch_a)
              + gates[:, :, 1] * jnp.einsum('bsc,cd->bsd', o_b, w_branch_b)
              + gates[:, :, 2] * jnp.einsum('bsc,cd->bsd', o_c, w_branch_c))
    mix = jnp.einsum('bsd,de->bse', merged, w_o)
    x = layer_norm(ALPHA * x + mix, ln1_g, ln1_b)
    up = causal_dwconv(jnp.einsum('bsd,df->bsf', x, w_up), ffn_conv_w)
    a, b = jnp.split(up, 2, axis=-1)
    ffn = jnp.einsum('bsf,fd->bsd', jax.nn.silu(a) * b, w_down)
    return layer_norm(ALPHA * x + ffn, ln2_g, ln2_b)


def setup_inputs(seed: int = 0) -> dict:
    key = jax.random.key(seed)
    ks = jax.random.split(key, 20)
    nrm = lambda k, shape, scale: jax.random.normal(k, shape, jnp.float32) * scale
    x = nrm(ks[0], (BATCH, SEQ, D_MODEL), 1.0)
    offset = jax.random.randint(ks[1], (BATCH, 1), 0, 1024, dtype=jnp.int32)
    positions = offset + jnp.arange(SEQ, dtype=jnp.int32)[None, :]
    return {
        "x": x,
        "positions": positions,
        "w_in": nrm(ks[2], (DEPTH, D_MODEL, IN_WIDTH), D_MODEL ** -0.5),
        "w_pool": nrm(ks[3], (DEPTH, N_POOL_GROUPS, POOL_GROUP_DIM, POOL_GROUP_DIM), POOL_GROUP_DIM ** -0.5),
        "pool_scale": 1.0 + nrm(ks[4], (DEPTH, POOL_WIDTH), 0.1),
        "attn_sinks": nrm(ks[5], (DEPTH, N_Q_HEADS), 0.5),
        "conv_w": nrm(ks[6], (DEPTH, CONV_K, CONV_WIDTH), CONV_K ** -0.5),
        "w_branch_a": nrm(ks[7], (DEPTH, POOL_WIDTH, D_MODEL), POOL_WIDTH ** -0.5),
        "w_branch_b": nrm(ks[8], (DEPTH, Q_WIDTH, D_MODEL), Q_WIDTH ** -0.5),
        "w_branch_c": nrm(ks[9], (DEPTH, CONV_WIDTH, D_MODEL), CONV_WIDTH ** -0.5),
        "w_o": nrm(ks[10], (DEPTH, D_MODEL, D_MODEL), BETA * D_MODEL ** -0.5),
        "ln1_g": 1.0 + nrm(ks[11], (DEPTH, D_MODEL), 0.02),
        "ln1_b": nrm(ks[12], (DEPTH, D_MODEL), 0.02),
        "w_up": nrm(ks[13], (DEPTH, D_MODEL, 2 * D_FF), D_MODEL ** -0.5),
        "ffn_conv_w": nrm(ks[14], (DEPTH, FFN_K, 2 * D_FF), FFN_K ** -0.5),
        "w_down": nrm(ks[15], (DEPTH, D_FF, D_MODEL), BETA * D_FF ** -0.5),
        "ln2_g": 1.0 + nrm(ks[16], (DEPTH, D_MODEL), 0.02),
        "ln2_b": nrm(ks[17], (DEPTH, D_MODEL), 0.02),
    }


def reference(x, positions, w_in, w_pool, pool_scale, attn_sinks, conv_w,
              w_branch_a, w_branch_b, w_branch_c, w_o, ln1_g, ln1_b,
              w_up, ffn_conv_w, w_down, ln2_g, ln2_b):
    cos, sin = rope_tables(positions)
    for l in range(DEPTH):
        x = hybrid_layer(x, cos, sin, w_in[l], w_pool[l], pool_scale[l], attn_sinks[l],
                         conv_w[l], w_branch_a[l], w_branch_b[l], w_branch_c[l], w_o[l],
                         ln1_g[l], ln1_b[l], w_up[l], ffn_conv_w[l], w_down[l],
                         ln2_g[l], ln2_b[l])
    return x
```

```python
import functools

import jax
import jax.numpy as jnp
from jax import lax
from jax.experimental import pallas as pl
from jax.experimental.pallas import tpu as pltpu

D_MODEL = 1024
DEPTH = 4
POOL_WINDOWS = (2, 4, 8, 16)
N_POOL_GROUPS = 4
POOL_GROUP_DIM = 64
POOL_WIDTH = 256
HEAD_DIM = 64
N_Q_HEADS = 8
N_KV_HEADS = 2
Q_WIDTH = 512
KV_WIDTH = 128
WINDOW = 128
ROT_DIM = 16
ROPE_THETA = 500000.0
CONV_WIDTH = 256
D_FF = 2816
ALPHA = (2 * DEPTH) ** 0.25
LN_EPS = 1e-5
MASK_VALUE = -1e30

COL_POOL = 0
COL_Q = COL_POOL + POOL_WIDTH
COL_K = COL_Q + Q_WIDTH
COL_V = COL_K + KV_WIDTH
COL_XC = COL_V + KV_WIDTH
COL_GATES = COL_XC + 3 * CONV_WIDTH
IN_WIDTH = COL_GATES + 3 * D_MODEL

LANES = 128
TILE = 512
POOL_HALO = 16
CONV_HALO = 8
FFN_CHUNK = 256
N_FFN_CHUNKS = D_FF // FFN_CHUNK
MERGE_CHUNK = 512
VMEM_LIMIT_BYTES = 56 * 1024 * 1024

BF16 = jnp.bfloat16
F32 = jnp.float32


def _dot(a, b):
    return jnp.dot(a, b, preferred_element_type=F32)


def _dot_nt(a, b):
    return lax.dot_general(a, b, (((1,), (1,)), ((), ())), preferred_element_type=F32)


def _sigmoid(x):
    return 0.5 * jnp.tanh(0.5 * x) + 0.5


def _layer_norm(y, g, b):
    mu = jnp.mean(y, axis=-1, keepdims=True)
    yc = y - mu
    var = jnp.mean(yc * yc, axis=-1, keepdims=True)
    return yc * lax.rsqrt(var + LN_EPS) * g + b


def mixer_kernel(x_ref, cos_ref, sin_ref, bias_ref, sinks_ref, win_ref, wpool_ref,
                 pscale_ref, convw_ref, wa_ref, wb_ref, wc_ref, wo_ref, g_ref, b_ref,
                 o_ref, ubuf, zbuf, kvbuf, qbuf, obuf, mbuf):
    T = TILE
    nblk = T // WINDOW
    i = pl.program_id(1)

    @pl.when(i == 0)
    def _():
        ubuf[0:POOL_HALO, :] = jnp.zeros((POOL_HALO, POOL_WIDTH), F32)
        zbuf[0:CONV_HALO, :] = jnp.zeros((CONV_HALO, CONV_WIDTH), F32)
        kvbuf[:, 0:WINDOW, :] = jnp.zeros((8, WINDOW, LANES), BF16)

    xf = x_ref[0]
    xb = xf.astype(BF16)

    lane = lax.broadcasted_iota(jnp.int32, (T, LANES), 1)
    low = lane < HEAD_DIM

    H = POOL_HALO
    ubuf[H:H + T, :] = _dot(xb, win_ref[:, COL_POOL:COL_POOL + POOL_WIDTH])
    u0 = ubuf[H:H + T, 0:LANES]
    u1 = ubuf[H:H + T, LANES:2 * LANES]
    s2 = u0 + ubuf[H - 1:H - 1 + T, 0:LANES]
    s4 = s2 + ubuf[H - 2:H - 2 + T, 0:LANES] + ubuf[H - 3:H - 3 + T, 0:LANES]
    e8 = ubuf[H - 8:H + T, LANES:2 * LANES]
    for j in range(1, 8):
        e8 = e8 + ubuf[H - 8 - j:H + T - j, LANES:2 * LANES]
    s8 = e8[8:, :]
    s16 = s8 + e8[:T, :]
    tpos1 = lax.broadcasted_iota(jnp.int32, (T, LANES), 0) + (i * T + 1)
    cnt_a = jnp.where(low, jnp.minimum(tpos1, 2), jnp.minimum(tpos1, 4)).astype(F32)
    cnt_b = jnp.where(low, jnp.minimum(tpos1, 8), jnp.minimum(tpos1, 16)).astype(F32)
    pooled_a = jnp.where(low, s2, s4) / cnt_a - u0
    pooled_b = jnp.where(low, s8, s16) / cnt_b - u1
    pooled = jnp.concatenate([pooled_a, pooled_b], axis=1).astype(BF16)
    o_a = (_dot(pooled, wpool_ref[...]) * pscale_ref[...]).astype(BF16)
    ubuf[0:H, :] = ubuf[T:T + H, :]

    Z = CONV_HALO
    cc = _dot(xb, win_ref[:, COL_XC:COL_XC + 3 * CONV_WIDTH])
    xc = cc[:, 0:CONV_WIDTH]
    gate_b = cc[:, CONV_WIDTH:2 * CONV_WIDTH]
    gate_c = cc[:, 2 * CONV_WIDTH:3 * CONV_WIDTH]
    z = gate_c * xc
    zbuf[Z:Z + T, :] = z
    cw = convw_ref[...]
    conv = zbuf[Z - 2:Z - 2 + T, :] * cw[0:1, :] + zbuf[Z - 1:Z - 1 + T, :] * cw[1:2, :] + z * cw[2:3, :]
    o_c = (gate_b * conv).astype(BF16)
    zbuf[0:Z, :] = zbuf[T:T + Z, :]

    qkv = _dot(xb, win_ref[:, COL_Q:COL_XC])
    cosv = cos_ref[0]
    sinv = sin_ref[0]
    rot_low = (lane & (HEAD_DIM - 1)) < (ROT_DIM // 2)

    def rope(t):
        partner = jnp.where(rot_low, pltpu.roll(t, LANES - ROT_DIM // 2, axis=1),
                            pltpu.roll(t, ROT_DIM // 2, axis=1))
        return t * cosv + partner * sinv

    for m in range(N_Q_HEADS // 2):
        qm = (rope(qkv[:, m * LANES:(m + 1) * LANES]) * (HEAD_DIM ** -0.5)).astype(BF16)
        for n in range(nblk):
            qbuf[m // 2, n, (m % 2) * WINDOW:(m % 2 + 1) * WINDOW, :] = qm[n * WINDOW:(n + 1) * WINDOW, :]

    kr = rope(qkv[:, Q_WIDTH:Q_WIDTH + KV_WIDTH])
    vv = qkv[:, Q_WIDTH + KV_WIDTH:Q_WIDTH + 2 * KV_WIDTH]
    zero = jnp.zeros((T, LANES), F32)
    for base, t in ((0, kr), (4, vv)):
        tr = pltpu.roll(t, HEAD_DIM, axis=1)
        kvbuf[base + 0, WINDOW:WINDOW + T, :] = jnp.where(low, t, zero).astype(BF16)
        kvbuf[base + 1, WINDOW:WINDOW + T, :] = jnp.where(low, zero, tr).astype(BF16)
        kvbuf[base + 2, WINDOW:WINDOW + T, :] = jnp.where(low, tr, zero).astype(BF16)
        kvbuf[base + 3, WINDOW:WINDOW + T, :] = jnp.where(low, zero, t).astype(BF16)

    low_w = lax.broadcasted_iota(jnp.int32, (WINDOW, LANES), 1) < HEAD_DIM
    first = jnp.where(i == 0, 1, 0)
    for n in range(nblk):
        bias = bias_ref[first] if n == 0 else bias_ref[0]
        seg = pl.ds(n * WINDOW, 2 * WINDOW)
        for h in range(N_KV_HEADS):
            qs = qbuf[h, n]
            s_lo = _dot_nt(qs, kvbuf[2 * h, seg, :])
            s_hi = _dot_nt(qs, kvbuf[2 * h + 1, seg, :])
            es = {}
            rs = {}
            for half, s_all in ((0, s_lo), (1, s_hi)):
                for j in range(2):
                    head = 4 * h + 2 * j + half
                    s = s_all[j * WINDOW:(j + 1) * WINDOW, :] + bias
                    sink = sinks_ref[head]
                    mx = jnp.maximum(jnp.max(s, axis=-1, keepdims=True), sink)
                    e = jnp.exp(s - mx)
                    den = jnp.sum(e, axis=-1, keepdims=True) + jnp.exp(sink - mx)
                    es[(half, j)] = e.astype(BF16)
                    rs[(half, j)] = 1.0 / den
            p_lo = jnp.concatenate([es[(0, 0)], es[(0, 1)]], axis=0)
            p_hi = jnp.concatenate([es[(1, 0)], es[(1, 1)]], axis=0)
            o = _dot(p_lo, kvbuf[4 + 2 * h, seg, :]) + _dot(p_hi, kvbuf[4 + 2 * h + 1, seg, :])
            for j in range(2):
                oj = o[j * WINDOW:(j + 1) * WINDOW, :]
                col = (2 * h + j) * LANES
                obuf[n * WINDOW:(n + 1) * WINDOW, col:col + LANES] = jnp.where(
                    low_w, oj * rs[(0, j)], oj * rs[(1, j)]).astype(BF16)
    kvbuf[:, 0:WINDOW, :] = kvbuf[:, T:T + WINDOW, :]
    o_b = obuf[...]

    for c0 in range(0, D_MODEL, MERGE_CHUNK):
        cs = slice(c0, c0 + MERGE_CHUNK)
        acc = None
        for br, (o_br, w_ref) in enumerate(((o_a, wa_ref), (o_b, wb_ref), (o_c, wc_ref))):
            g0 = COL_GATES + br * D_MODEL + c0
            gate = _sigmoid(_dot(xb, win_ref[:, g0:g0 + MERGE_CHUNK]))
            term = gate * _dot(o_br, w_ref[:, cs])
            acc = term if acc is None else acc + term
        mbuf[:, cs] = acc.astype(BF16)
    mix = _dot(mbuf[...], wo_ref[...])
    o_ref[0] = _layer_norm(ALPHA * xf + mix, g_ref[...], b_ref[...])


def ffn_kernel(x_ref, wup_ref, cw_ref, wdown_ref, g_ref, b_ref, o_ref, cbuf, carry, hbuf):
    T = TILE
    Z = CONV_HALO
    i = pl.program_id(1)

    @pl.when(i == 0)
    def _():
        carry[...] = jnp.zeros(carry.shape, F32)

    xf = x_ref[0]
    xb = xf.astype(BF16)
    for c in range(N_FFN_CHUNKS):
        cols = slice(c * 2 * FFN_CHUNK, (c + 1) * 2 * FFN_CHUNK)
        slot = c % 2
        up = _dot(xb, wup_ref[:, cols])
        cbuf[slot, 0:Z, :] = carry[:, cols]
        cbuf[slot, Z:Z + T, :] = up
        cw = cw_ref[:, cols]
        y = (cbuf[slot, Z - 2:Z - 2 + T, :] * cw[0:1, :] + cbuf[slot, Z - 1:Z - 1 + T, :] * cw[1:2, :]
             + up * cw[2:3, :])
        carry[:, cols] = cbuf[slot, T:T + Z, :]
        a = y[:, 0:FFN_CHUNK]
        b = y[:, FFN_CHUNK:2 * FFN_CHUNK]
        hbuf[:, c * FFN_CHUNK:(c + 1) * FFN_CHUNK] = (a * _sigmoid(a) * b).astype(BF16)
    ffn = _dot(hbuf[...], wdown_ref[...])
    o_ref[0] = _layer_norm(ALPHA * xf + ffn, g_ref[...], b_ref[...])


def _resident(shape, layer):
    nd = len(shape)
    return pl.BlockSpec((None,) + tuple(shape), lambda b, i: (layer,) + (0,) * nd,
                        pipeline_mode=pl.Buffered(1))


def _compiler_params():
    return pltpu.CompilerParams(dimension_semantics=("arbitrary", "arbitrary"),
                                vmem_limit_bytes=VMEM_LIMIT_BYTES)


def _mixer_call(layer, batch, seq):
    T = TILE
    tok = lambda width: pl.BlockSpec((1, T, width), lambda b, i: (b, i, 0))
    in_specs = [
        tok(D_MODEL), tok(LANES), tok(LANES),
        pl.BlockSpec((2, WINDOW, 2 * WINDOW), lambda b, i: (0, 0, 0)),
        pl.BlockSpec(memory_space=pltpu.SMEM),
        _resident((D_MODEL, IN_WIDTH), layer),
        _resident((POOL_WIDTH, POOL_WIDTH), layer),
        _resident((1, POOL_WIDTH), layer),
        _resident((3, CONV_WIDTH), layer),
        _resident((POOL_WIDTH, D_MODEL), layer),
        _resident((Q_WIDTH, D_MODEL), layer),
        _resident((CONV_WIDTH, D_MODEL), layer),
        _resident((D_MODEL, D_MODEL), layer),
        _resident((1, D_MODEL), layer),
        _resident((1, D_MODEL), layer),
    ]
    scratch = [
        pltpu.VMEM((POOL_HALO + T, POOL_WIDTH), F32),
        pltpu.VMEM((CONV_HALO + T, CONV_WIDTH), F32),
        pltpu.VMEM((8, WINDOW + T, LANES), BF16),
        pltpu.VMEM((N_KV_HEADS, T // WINDOW, 2 * WINDOW, LANES), BF16),
        pltpu.VMEM((T, Q_WIDTH), BF16),
        pltpu.VMEM((T, D_MODEL), BF16),
    ]
    return pl.pallas_call(
        mixer_kernel,
        grid=(batch, seq // T),
        in_specs=in_specs,
        out_specs=tok(D_MODEL),
        out_shape=jax.ShapeDtypeStruct((batch, seq, D_MODEL), F32),
        scratch_shapes=scratch,
        compiler_params=_compiler_params(),
        name=f"mixer_l{layer}",
    )


def _ffn_call(layer, batch, seq):
    T = TILE
    tok = pl.BlockSpec((1, T, D_MODEL), lambda b, i: (b, i, 0))
    in_specs = [
        tok,
        _resident((D_MODEL, 2 * D_FF), layer),
        _resident((3, 2 * D_FF), layer),
        _resident((D_FF, D_MODEL), layer),
        _resident((1, D_MODEL), layer),
        _resident((1, D_MODEL), layer),
    ]
    scratch = [
        pltpu.VMEM((2, CONV_HALO + T, 2 * FFN_CHUNK), F32),
        pltpu.VMEM((CONV_HALO, 2 * D_FF), F32),
        pltpu.VMEM((T, D_FF), BF16),
    ]
    return pl.pallas_call(
        ffn_kernel,
        grid=(batch, seq // T),
        in_specs=in_specs,
        out_specs=tok,
        out_shape=jax.ShapeDtypeStruct((batch, seq, D_MODEL), F32),
        scratch_shapes=scratch,
        compiler_params=_compiler_params(),
        name=f"ffn_l{layer}",
    )


def _interleave_ffn_chunks(w):
    lead = w.shape[:-1]
    a = w[..., :D_FF].reshape(lead + (N_FFN_CHUNKS, 1, FFN_CHUNK))
    b = w[..., D_FF:].reshape(lead + (N_FFN_CHUNKS, 1, FFN_CHUNK))
    return jnp.concatenate([a, b], axis=-2).reshape(lead + (2 * D_FF,))


def _rope_lane_tables(positions):
    inv_freq = ROPE_THETA ** (-jnp.arange(0, ROT_DIM, 2, dtype=F32) / ROT_DIM)
    ang = positions.astype(F32)[..., None] * inv_freq
    cos, sin = jnp.cos(ang), jnp.sin(ang)
    pad = HEAD_DIM - ROT_DIM
    ones = jnp.ones(cos.shape[:-1] + (pad,), F32)
    zeros = jnp.zeros(cos.shape[:-1] + (pad,), F32)
    cos_h = jnp.concatenate([cos, cos, ones], axis=-1)
    sin_h = jnp.concatenate([-sin, sin, zeros], axis=-1)
    return jnp.concatenate([cos_h, cos_h], axis=-1), jnp.concatenate([sin_h, sin_h], axis=-1)


def _band_bias():
    qi = jnp.arange(WINDOW)[:, None]
    kj = jnp.arange(2 * WINDOW)[None, :]
    band = (kj > qi) & (kj <= qi + WINDOW)
    start = band & (kj >= WINDOW)
    as_bias = lambda valid: jnp.where(valid, 0.0, MASK_VALUE).astype(F32)
    return jnp.stack([as_bias(band), as_bias(start)])


def kernel(x, positions, w_in, w_pool, pool_scale, attn_sinks, conv_w, w_branch_a, w_branch_b,
           w_branch_c, w_o, ln1_g, ln1_b, w_up, ffn_conv_w, w_down, ln2_g, ln2_b):
    batch, seq, _ = x.shape
    assert seq % TILE == 0 and TILE % WINDOW == 0
    cos_t, sin_t = _rope_lane_tables(positions)
    bias = _band_bias()

    eye = jnp.eye(N_POOL_GROUPS, dtype=w_pool.dtype)
    wpool_bd = jnp.einsum('lgcd,gh->lgchd', w_pool, eye).reshape(DEPTH, POOL_WIDTH, POOL_WIDTH)

    row = lambda v: v.reshape(DEPTH, 1, v.shape[-1])
    w_in_b = w_in.astype(BF16)
    wpool_b = wpool_bd.astype(BF16)
    wa_b, wb_b, wc_b = w_branch_a.astype(BF16), w_branch_b.astype(BF16), w_branch_c.astype(BF16)
    wo_b = w_o.astype(BF16)
    wup_b = _interleave_ffn_chunks(w_up).astype(BF16)
    ffn_cw = _interleave_ffn_chunks(ffn_conv_w)
    wdown_b = w_down.astype(BF16)

    for l in range(DEPTH):
        x = _mixer_call(l, batch, seq)(
            x, cos_t, sin_t, bias, attn_sinks[l], w_in_b, wpool_b, row(pool_scale), conv_w,
            wa_b, wb_b, wc_b, wo_b, row(ln1_g), row(ln1_b))
        x = _ffn_call(l, batch, seq)(
            x, wup_b, ffn_cw, wdown_b, row(ln2_g), row(ln2_b))
    return x
```

```python
import functools

import jax
import jax.numpy as jnp
from jax import lax
from jax.experimental import pallas as pl
from jax.experimental.pallas import tpu as pltpu

D_MODEL = 1024
DEPTH = 4
POOL_WINDOWS = (2, 4, 8, 16)
N_POOL_GROUPS = 4
POOL_GROUP_DIM = 64
POOL_WIDTH = 256
HEAD_DIM = 64
N_Q_HEADS = 8
N_KV_HEADS = 2
Q_WIDTH = 512
KV_WIDTH = 128
WINDOW = 128
ROT_DIM = 16
ROPE_THETA = 500000.0
CONV_WIDTH = 256
D_FF = 2816
ALPHA = (2 * DEPTH) ** 0.25
LN_EPS = 1e-5
MASK_VALUE = -1e30

COL_POOL = 0
COL_Q = COL_POOL + POOL_WIDTH
COL_K = COL_Q + Q_WIDTH
COL_V = COL_K + KV_WIDTH
COL_XC = COL_V + KV_WIDTH
COL_GATES = COL_XC + 3 * CONV_WIDTH
IN_WIDTH = COL_GATES + 3 * D_MODEL

LANES = 128
SUBLANES = 8
LAYER_TILES = ((512, 512), (1024, 1024), (256, 256), (512, 512))
ROPE_TILE = 512
POOL_HALO = 16
CONV_HALO = SUBLANES
FFN_CHUNK = 256
N_FFN_CHUNKS = D_FF // FFN_CHUNK
MERGE_CHUNK = 512
VMEM_LIMIT_BYTES = 58 * 1024 * 1024

BF16 = jnp.bfloat16
F32 = jnp.float32


def _dot(a, b):
    return jnp.dot(a, b, preferred_element_type=F32)


def _dot_nt(a, b):
    return lax.dot_general(a, b, (((1,), (1,)), ((), ())), preferred_element_type=F32)


def _sigmoid(x):
    return 0.5 * jnp.tanh(0.5 * x) + 0.5


def _layer_norm(y, g, b):
    mu = jnp.mean(y, axis=-1, keepdims=True)
    yc = y - mu
    var = jnp.mean(yc * yc, axis=-1, keepdims=True)
    return yc * lax.rsqrt(var + LN_EPS) * g + b


def rope_table_kernel(pos_ref, invf_ref, cos_ref, sin_ref):
    T = ROPE_TILE
    half = ROT_DIM // 2
    ang = pos_ref[0].astype(F32) * invf_ref[...]
    c8 = jnp.cos(ang)
    s8 = jnp.sin(ang)
    ones = jnp.ones((HEAD_DIM - ROT_DIM, T), F32)
    zeros = jnp.zeros((HEAD_DIM - ROT_DIM, T), F32)
    cos_rows = jnp.concatenate([c8, c8, ones, c8, c8, ones], axis=0)
    sin_rows = jnp.concatenate([-s8, s8, zeros, -s8, s8, zeros], axis=0)
    cos_ref[0] = cos_rows.T
    sin_ref[0] = sin_rows.T


def copy_kernel(x_ref, o_ref):
    o_ref[...] = x_ref[...]


def mixer_kernel(x_ref, cos_ref, sin_ref, bias_ref, sinks_ref, win_ref, wpool_ref,
                 pscale_ref, convw_ref, wa_ref, wb_ref, wc_ref, wo_ref, g_ref, b_ref,
                 o_ref, ubuf, zbuf, kvbuf, qbuf, obuf, mbuf, *, T):
    nblk = T // WINDOW
    i = pl.program_id(1)

    @pl.when(i == 0)
    def _():
        ubuf[0:POOL_HALO, :] = jnp.zeros((POOL_HALO, POOL_WIDTH), F32)
        zbuf[0:CONV_HALO, :] = jnp.zeros((CONV_HALO, CONV_WIDTH), F32)
        kvbuf[:, 0:WINDOW, :] = jnp.zeros((8, WINDOW, LANES), BF16)

    xf = x_ref[0]
    xb = xf.astype(BF16)

    lane = lax.broadcasted_iota(jnp.int32, (T, LANES), 1)
    low = lane < HEAD_DIM

    H = POOL_HALO
    ubuf[H:H + T, :] = _dot(xb, win_ref[:, COL_POOL:COL_POOL + POOL_WIDTH])
    u0 = ubuf[H:H + T, 0:LANES]
    u1 = ubuf[H:H + T, LANES:2 * LANES]
    s2 = u0 + ubuf[H - 1:H - 1 + T, 0:LANES]
    s4 = s2 + ubuf[H - 2:H - 2 + T, 0:LANES] + ubuf[H - 3:H - 3 + T, 0:LANES]
    e8 = ubuf[H - 8:H + T, LANES:2 * LANES]
    for j in range(1, 8):
        e8 = e8 + ubuf[H - 8 - j:H + T - j, LANES:2 * LANES]
    s8 = e8[8:, :]
    s16 = s8 + e8[:T, :]
    tpos1 = lax.broadcasted_iota(jnp.int32, (T, LANES), 0) + (i * T + 1)
    cnt_a = jnp.where(low, jnp.minimum(tpos1, 2), jnp.minimum(tpos1, 4)).astype(F32)
    cnt_b = jnp.where(low, jnp.minimum(tpos1, 8), jnp.minimum(tpos1, 16)).astype(F32)
    pooled_a = jnp.where(low, s2, s4) / cnt_a - u0
    pooled_b = jnp.where(low, s8, s16) / cnt_b - u1
    pooled = jnp.concatenate([pooled_a, pooled_b], axis=1).astype(BF16)
    o_a = (_dot(pooled, wpool_ref[...]) * pscale_ref[...]).astype(BF16)
    ubuf[0:H, :] = ubuf[T:T + H, :]

    Z = CONV_HALO
    cc = _dot(xb, win_ref[:, COL_XC:COL_XC + 3 * CONV_WIDTH])
    xc = cc[:, 0:CONV_WIDTH]
    gate_b = cc[:, CONV_WIDTH:2 * CONV_WIDTH]
    gate_c = cc[:, 2 * CONV_WIDTH:3 * CONV_WIDTH]
    z = gate_c * xc
    zbuf[Z:Z + T, :] = z
    cw = convw_ref[...]
    conv = zbuf[Z - 2:Z - 2 + T, :] * cw[0:1, :] + zbuf[Z - 1:Z - 1 + T, :] * cw[1:2, :] + z * cw[2:3, :]
    o_c = (gate_b * conv).astype(BF16)
    zbuf[0:Z, :] = zbuf[T:T + Z, :]

    qkv = _dot(xb, win_ref[:, COL_Q:COL_XC])
    cosv = cos_ref[0]
    sinv = sin_ref[0]
    rot_low = (lane & (HEAD_DIM - 1)) < (ROT_DIM // 2)

    def rope(t):
        partner = jnp.where(rot_low, pltpu.roll(t, LANES - ROT_DIM // 2, axis=1),
                            pltpu.roll(t, ROT_DIM // 2, axis=1))
        return t * cosv + partner * sinv

    for m in range(N_Q_HEADS // 2):
        qm = (rope(qkv[:, m * LANES:(m + 1) * LANES]) * (HEAD_DIM ** -0.5)).astype(BF16)
        for n in range(nblk):
            qbuf[m // 2, n, (m % 2) * WINDOW:(m % 2 + 1) * WINDOW, :] = qm[n * WINDOW:(n + 1) * WINDOW, :]

    kr = rope(qkv[:, Q_WIDTH:Q_WIDTH + KV_WIDTH])
    vv = qkv[:, Q_WIDTH + KV_WIDTH:Q_WIDTH + 2 * KV_WIDTH]
    zero = jnp.zeros((T, LANES), F32)
    for base, t in ((0, kr), (4, vv)):
        tr = pltpu.roll(t, HEAD_DIM, axis=1)
        kvbuf[base + 0, WINDOW:WINDOW + T, :] = jnp.where(low, t, zero).astype(BF16)
        kvbuf[base + 1, WINDOW:WINDOW + T, :] = jnp.where(low, zero, tr).astype(BF16)
        kvbuf[base + 2, WINDOW:WINDOW + T, :] = jnp.where(low, tr, zero).astype(BF16)
        kvbuf[base + 3, WINDOW:WINDOW + T, :] = jnp.where(low, zero, t).astype(BF16)

    low_w = lax.broadcasted_iota(jnp.int32, (WINDOW, LANES), 1) < HEAD_DIM
    first = jnp.where(i == 0, 1, 0)
    for n in range(nblk):
        bias = bias_ref[first] if n == 0 else bias_ref[0]
        seg = pl.ds(n * WINDOW, 2 * WINDOW)
        for h in range(N_KV_HEADS):
            qs = qbuf[h, n]
            s_lo = _dot_nt(qs, kvbuf[2 * h, seg, :])
            s_hi = _dot_nt(qs, kvbuf[2 * h + 1, seg, :])
            es = {}
            rs = {}
            for half, s_all in ((0, s_lo), (1, s_hi)):
                for j in range(2):
                    head = 4 * h + 2 * j + half
                    s = s_all[j * WINDOW:(j + 1) * WINDOW, :] + bias
                    sink = sinks_ref[head]
                    mx = jnp.maximum(jnp.max(s, axis=-1, keepdims=True), sink)
                    e = jnp.exp(s - mx)
                    den = jnp.sum(e, axis=-1, keepdims=True) + jnp.exp(sink - mx)
                    es[(half, j)] = e.astype(BF16)
                    rs[(half, j)] = 1.0 / den
            p_lo = jnp.concatenate([es[(0, 0)], es[(0, 1)]], axis=0)
            p_hi = jnp.concatenate([es[(1, 0)], es[(1, 1)]], axis=0)
            o = _dot(p_lo, kvbuf[4 + 2 * h, seg, :]) + _dot(p_hi, kvbuf[4 + 2 * h + 1, seg, :])
            for j in range(2):
                oj = o[j * WINDOW:(j + 1) * WINDOW, :]
                col = (2 * h + j) * LANES
                obuf[n * WINDOW:(n + 1) * WINDOW, col:col + LANES] = jnp.where(
                    low_w, oj * rs[(0, j)], oj * rs[(1, j)]).astype(BF16)
    kvbuf[:, 0:WINDOW, :] = kvbuf[:, T:T + WINDOW, :]
    o_b = obuf[...]

    for c0 in range(0, D_MODEL, MERGE_CHUNK):
        cs = slice(c0, c0 + MERGE_CHUNK)
        acc = None
        for br, (o_br, w_ref) in enumerate(((o_a, wa_ref), (o_b, wb_ref), (o_c, wc_ref))):
            g0 = COL_GATES + br * D_MODEL + c0
            gate = _sigmoid(_dot(xb, win_ref[:, g0:g0 + MERGE_CHUNK]))
            term = gate * _dot(o_br, w_ref[:, cs])
            acc = term if acc is None else acc + term
        mbuf[:, cs] = acc.astype(BF16)
    mix = _dot(mbuf[...], wo_ref[...])
    o_ref[0] = _layer_norm(ALPHA * xf + mix, g_ref[...], b_ref[...])


def ffn_kernel(x_ref, wup_ref, cw_ref, wdown_ref, g_ref, b_ref, o_ref, cbuf, carry, hbuf, *, T):
    Z = CONV_HALO
    C = FFN_CHUNK
    i = pl.program_id(1)

    @pl.when(i == 0)
    def _():
        carry[...] = jnp.zeros(carry.shape, F32)

    xf = x_ref[0]
    xb = xf.astype(BF16)
    for c in range(N_FFN_CHUNKS):
        slot = c % 2
        ys = []
        for part in range(2):
            cols = slice(part * D_FF + c * C, part * D_FF + (c + 1) * C)
            here = slice(part * C, (part + 1) * C)
            up = _dot(xb, wup_ref[:, cols])
            cbuf[slot, 0:Z, here] = carry[:, cols]
            cbuf[slot, Z:Z + T, here] = up
            cw = cw_ref[:, cols]
            ys.append(cbuf[slot, Z - 2:Z - 2 + T, here] * cw[0:1, :]
                      + cbuf[slot, Z - 1:Z - 1 + T, here] * cw[1:2, :] + up * cw[2:3, :])
            carry[:, cols] = cbuf[slot, T:T + Z, here]
        a, b = ys
        hbuf[:, c * C:(c + 1) * C] = (a * _sigmoid(a) * b).astype(BF16)
    ffn = _dot(hbuf[...], wdown_ref[...])
    o_ref[0] = _layer_norm(ALPHA * xf + ffn, g_ref[...], b_ref[...])


def _resident(shape, layer):
    nd = len(shape)
    return pl.BlockSpec((None,) + tuple(shape), lambda b, i: (layer,) + (0,) * nd,
                        pipeline_mode=pl.Buffered(1))


def _compiler_params():
    return pltpu.CompilerParams(dimension_semantics=("arbitrary", "arbitrary"),
                                vmem_limit_bytes=VMEM_LIMIT_BYTES)


def _rope_call(batch, seq):
    T = ROPE_TILE
    out = pl.BlockSpec((1, T, LANES), lambda b, i: (b, i, 0))
    return pl.pallas_call(
        rope_table_kernel,
        grid=(batch, seq // T),
        in_specs=[pl.BlockSpec((1, 1, T), lambda b, i: (b, 0, i)),
                  pl.BlockSpec((ROT_DIM // 2, T), lambda b, i: (0, 0))],
        out_specs=[out, out],
        out_shape=[jax.ShapeDtypeStruct((batch, seq, LANES), F32)] * 2,
        name="rope_table",
    )


def _copy_call(batch, seq, T, name):
    spec = pl.BlockSpec((1, T, D_MODEL), lambda b, i: (b, i, 0))
    return pl.pallas_call(
        copy_kernel, grid=(batch, seq // T), in_specs=[spec], out_specs=spec,
        out_shape=jax.ShapeDtypeStruct((batch, seq, D_MODEL), F32),
        compiler_params=_compiler_params(), name=name)


def _mixer_call(layer, batch, seq, T):
    tok = lambda width: pl.BlockSpec((1, T, width), lambda b, i: (b, i, 0))
    in_specs = [
        tok(D_MODEL), tok(LANES), tok(LANES),
        pl.BlockSpec((2, WINDOW, 2 * WINDOW), lambda b, i: (0, 0, 0)),
        pl.BlockSpec(memory_space=pltpu.SMEM),
        _resident((D_MODEL, IN_WIDTH), layer),
        _resident((POOL_WIDTH, POOL_WIDTH), layer),
        _resident((1, POOL_WIDTH), layer),
        _resident((3, CONV_WIDTH), layer),
        _resident((POOL_WIDTH, D_MODEL), layer),
        _resident((Q_WIDTH, D_MODEL), layer),
        _resident((CONV_WIDTH, D_MODEL), layer),
        _resident((D_MODEL, D_MODEL), layer),
        _resident((1, D_MODEL), layer),
        _resident((1, D_MODEL), layer),
    ]
    scratch = [
        pltpu.VMEM((POOL_HALO + T, POOL_WIDTH), F32),
        pltpu.VMEM((CONV_HALO + T, CONV_WIDTH), F32),
        pltpu.VMEM((8, WINDOW + T, LANES), BF16),
        pltpu.VMEM((N_KV_HEADS, T // WINDOW, 2 * WINDOW, LANES), BF16),
        pltpu.VMEM((T, Q_WIDTH), BF16),
        pltpu.VMEM((T, D_MODEL), BF16),
    ]
    return pl.pallas_call(
        functools.partial(mixer_kernel, T=T),
        grid=(batch, seq // T),
        in_specs=in_specs,
        out_specs=tok(D_MODEL),
        out_shape=jax.ShapeDtypeStruct((batch, seq, D_MODEL), F32),
        scratch_shapes=scratch,
        compiler_params=_compiler_params(),
        name=f"mixer_l{layer}",
    )


def _ffn_call(layer, batch, seq, T):
    tok = pl.BlockSpec((1, T, D_MODEL), lambda b, i: (b, i, 0))
    in_specs = [
        tok,
        _resident((D_MODEL, 2 * D_FF), layer),
        _resident((3, 2 * D_FF), layer),
        _resident((D_FF, D_MODEL), layer),
        _resident((1, D_MODEL), layer),
        _resident((1, D_MODEL), layer),
    ]
    scratch = [
        pltpu.VMEM((2, CONV_HALO + T, 2 * FFN_CHUNK), F32),
        pltpu.VMEM((CONV_HALO, 2 * D_FF), F32),
        pltpu.VMEM((T, D_FF), BF16),
    ]
    return pl.pallas_call(
        functools.partial(ffn_kernel, T=T),
        grid=(batch, seq // T),
        in_specs=in_specs,
        out_specs=tok,
        out_shape=jax.ShapeDtypeStruct((batch, seq, D_MODEL), F32),
        scratch_shapes=scratch,
        compiler_params=_compiler_params(),
        name=f"ffn_l{layer}",
    )


def _band_bias():
    qi = jnp.arange(WINDOW)[:, None]
    kj = jnp.arange(2 * WINDOW)[None, :]
    band = (kj > qi) & (kj <= qi + WINDOW)
    start = band & (kj >= WINDOW)
    as_bias = lambda valid: jnp.where(valid, 0.0, MASK_VALUE).astype(F32)
    return jnp.stack([as_bias(band), as_bias(start)])


def kernel(x, positions, w_in, w_pool, pool_scale, attn_sinks, conv_w, w_branch_a, w_branch_b,
           w_branch_c, w_o, ln1_g, ln1_b, w_up, ffn_conv_w, w_down, ln2_g, ln2_b):
    batch, seq, _ = x.shape
    inv_freq = ROPE_THETA ** (-jnp.arange(0, ROT_DIM, 2, dtype=F32) / ROT_DIM)
    inv_freq_rows = jnp.broadcast_to(inv_freq[:, None], (ROT_DIM // 2, ROPE_TILE))
    cos_t, sin_t = _rope_call(batch, seq)(positions.reshape(batch, 1, seq), inv_freq_rows)
    bias = _band_bias()

    eye = jnp.eye(N_POOL_GROUPS, dtype=w_pool.dtype)
    wpool_bd = jnp.einsum('lgcd,gh->lgchd', w_pool, eye).reshape(DEPTH, POOL_WIDTH, POOL_WIDTH)

    row = lambda v: v.reshape(DEPTH, 1, v.shape[-1])
    w_in_b = w_in.astype(BF16)
    wpool_b = wpool_bd.astype(BF16)
    wa_b, wb_b, wc_b = w_branch_a.astype(BF16), w_branch_b.astype(BF16), w_branch_c.astype(BF16)
    wo_b = w_o.astype(BF16)
    wup_b = w_up.astype(BF16)
    wdown_b = w_down.astype(BF16)

    for l in range(DEPTH):
        t_mix, t_ffn = LAYER_TILES[l]
        x = _mixer_call(l, batch, seq, t_mix)(
            x, cos_t, sin_t, bias, attn_sinks[l], w_in_b, wpool_b, row(pool_scale), conv_w,
            wa_b, wb_b, wc_b, wo_b, row(ln1_g), row(ln1_b))
        x = _ffn_call(l, batch, seq, t_ffn)(
            x, wup_b, ffn_conv_w, wdown_b, row(ln2_g), row(ln2_b))
    x = _copy_call(batch, seq, 512, "copy_t512")(x)
    x = _copy_call(batch, seq, 2048, "copy_t2048")(x)
    return x
```

```python
import functools

import jax
import jax.numpy as jnp
from jax import lax
from jax.experimental import pallas as pl
from jax.experimental.pallas import tpu as pltpu

D_MODEL = 1024
DEPTH = 4
N_POOL_GROUPS = 4
POOL_WIDTH = 256
HEAD_DIM = 64
N_Q_HEADS = 8
N_KV_HEADS = 2
Q_WIDTH = 512
KV_WIDTH = 128
WINDOW = 128
ROT_DIM = 16
ROPE_THETA = 500000.0
CONV_WIDTH = 256
D_FF = 2816
ALPHA = (2 * DEPTH) ** 0.25
LN_EPS = 1e-5
MASK_VALUE = -1e30

COL_POOL = 0
COL_Q = COL_POOL + POOL_WIDTH
COL_K = COL_Q + Q_WIDTH
COL_V = COL_K + KV_WIDTH
COL_XC = COL_V + KV_WIDTH
COL_GATES = COL_XC + 3 * CONV_WIDTH
IN_WIDTH = COL_GATES + 3 * D_MODEL

LANES = 128
SUBLANES = 8
LAYER_VARIANTS = (((512, 2, None), (512, 4, False)), ((512, 2, None), (512, 4, True)),
                  ((512, 2, None), (512, 4, False)), ((512, 2, None), (512, 4, True)))
ROPE_TILE = 2048
POOL_HALO = 16
CONV_HALO = SUBLANES
FFN_CHUNK = 256
N_FFN_CHUNKS = D_FF // FFN_CHUNK
MERGE_CHUNK = 512
GATE_BLOCK = 256
HEAD_GATE_BLOCKS = 4
VMEM_LIMIT_BYTES = 58 * 1024 * 1024

BF16 = jnp.bfloat16
F32 = jnp.float32


def _dot(a, b):
    return jnp.dot(a, b, preferred_element_type=F32)


def _dot_nt(a, b):
    return lax.dot_general(a, b, (((1,), (1,)), ((), ())), preferred_element_type=F32)


def _layer_norm(y, g, b):
    mu = jnp.mean(y, axis=-1, keepdims=True)
    yc = y - mu
    var = jnp.mean(yc * yc, axis=-1, keepdims=True)
    return yc * lax.rsqrt(var + LN_EPS) * g + b


def rope_table_kernel(pos_ref, invf_ref, cos_ref, sin_ref):
    T = ROPE_TILE
    ang = pos_ref[0].astype(F32) * invf_ref[...]
    c8 = jnp.cos(ang)
    s8 = jnp.sin(ang)
    ones = jnp.ones((HEAD_DIM - ROT_DIM, T), F32)
    zeros = jnp.zeros((HEAD_DIM - ROT_DIM, T), F32)
    cos_rows = jnp.concatenate([c8, c8, ones, c8, c8, ones], axis=0)
    sin_rows = jnp.concatenate([-s8, s8, zeros, -s8, s8, zeros], axis=0)
    cos_ref[0] = cos_rows.T
    sin_ref[0] = sin_rows.T


def mixer_kernel(x_ref, cos_ref, sin_ref, bias_ref, sinks_ref, win_ref, wpool_ref,
                 pscale_ref, convw_ref, wa_ref, wb_ref, wc_ref, wo_ref, g_ref, b_ref,
                 o_ref, ubuf, zbuf, kvbuf, qbuf, obuf, mbuf, gbuf, *, T, tail_blocks):
    nblk = T // WINDOW
    i = pl.program_id(1)

    @pl.when(i == 0)
    def _():
        ubuf[0:POOL_HALO, :] = jnp.zeros((POOL_HALO, POOL_WIDTH), F32)
        zbuf[0:CONV_HALO, :] = jnp.zeros((CONV_HALO, CONV_WIDTH), F32)
        kvbuf[:, 0:WINDOW, :] = jnp.zeros((8, WINDOW, LANES), BF16)

    xf = x_ref[0]
    xb = xf.astype(BF16)

    lane = lax.broadcasted_iota(jnp.int32, (T, LANES), 1)
    low = lane < HEAD_DIM

    n_units = nblk * N_KV_HEADS
    gate_blocks = [(COL_GATES + c, c) for c in range(0, 3 * D_MODEL, GATE_BLOCK)]
    rest = gate_blocks[HEAD_GATE_BLOCKS:]
    unit_blocks = [rest[(u * len(rest)) // n_units:((u + 1) * len(rest)) // n_units] for u in range(n_units)]

    def gate_pieces(blocks):
        for src, dst in blocks:
            gbuf[:, dst:dst + GATE_BLOCK] = jnp.tanh(_dot(xb, win_ref[:, src:src + GATE_BLOCK]))

    H = POOL_HALO
    ubuf[H:H + T, :] = _dot(xb, win_ref[:, COL_POOL:COL_POOL + POOL_WIDTH])
    cc = _dot(xb, win_ref[:, COL_XC:COL_XC + 3 * CONV_WIDTH])
    qkv = _dot(xb, win_ref[:, COL_Q:COL_XC])
    gate_pieces(gate_blocks[:HEAD_GATE_BLOCKS])

    u0 = ubuf[H:H + T, 0:LANES]
    u1 = ubuf[H:H + T, LANES:2 * LANES]
    s2 = u0 + ubuf[H - 1:H - 1 + T, 0:LANES]
    s4 = s2 + ubuf[H - 2:H - 2 + T, 0:LANES] + ubuf[H - 3:H - 3 + T, 0:LANES]
    e8 = ubuf[H - 8:H + T, LANES:2 * LANES]
    for j in range(1, 8):
        e8 = e8 + ubuf[H - 8 - j:H + T - j, LANES:2 * LANES]
    s8 = e8[8:, :]
    s16 = s8 + e8[:T, :]
    tpos1 = lax.broadcasted_iota(jnp.int32, (T, LANES), 0) + (i * T + 1)
    cnt_a = jnp.where(low, jnp.minimum(tpos1, 2), jnp.minimum(tpos1, 4)).astype(F32)
    cnt_b = jnp.where(low, jnp.minimum(tpos1, 8), jnp.minimum(tpos1, 16)).astype(F32)
    pooled_a = jnp.where(low, s2, s4) / cnt_a - u0
    pooled_b = jnp.where(low, s8, s16) / cnt_b - u1
    pooled = jnp.concatenate([pooled_a, pooled_b], axis=1).astype(BF16)
    o_a = (_dot(pooled, wpool_ref[...]) * pscale_ref[...]).astype(BF16)
    ubuf[0:H, :] = ubuf[T:T + H, :]

    Z = CONV_HALO
    xc = cc[:, 0:CONV_WIDTH]
    gate_b = cc[:, CONV_WIDTH:2 * CONV_WIDTH]
    gate_c = cc[:, 2 * CONV_WIDTH:3 * CONV_WIDTH]
    z = gate_c * xc
    zbuf[Z:Z + T, :] = z
    cw = convw_ref[...]
    conv = zbuf[Z - 2:Z - 2 + T, :] * cw[0:1, :] + zbuf[Z - 1:Z - 1 + T, :] * cw[1:2, :] + z * cw[2:3, :]
    o_c = (gate_b * conv).astype(BF16)
    zbuf[0:Z, :] = zbuf[T:T + Z, :]

    cosv = cos_ref[0]
    sinv = sin_ref[0]
    rot_low = (lane & (HEAD_DIM - 1)) < (ROT_DIM // 2)

    def rope(t):
        partner = jnp.where(rot_low, pltpu.roll(t, LANES - ROT_DIM // 2, axis=1),
                            pltpu.roll(t, ROT_DIM // 2, axis=1))
        return t * cosv + partner * sinv

    for m in range(N_Q_HEADS // 2):
        qm = (rope(qkv[:, m * LANES:(m + 1) * LANES]) * (HEAD_DIM ** -0.5)).astype(BF16)
        for n in range(nblk):
            qbuf[m // 2, n, (m % 2) * WINDOW:(m % 2 + 1) * WINDOW, :] = qm[n * WINDOW:(n + 1) * WINDOW, :]

    kr = rope(qkv[:, Q_WIDTH:Q_WIDTH + KV_WIDTH])
    vv = qkv[:, Q_WIDTH + KV_WIDTH:Q_WIDTH + 2 * KV_WIDTH]
    zero = jnp.zeros((T, LANES), F32)
    for base, t in ((0, kr), (4, vv)):
        tr = pltpu.roll(t, HEAD_DIM, axis=1)
        kvbuf[base + 0, WINDOW:WINDOW + T, :] = jnp.where(low, t, zero).astype(BF16)
        kvbuf[base + 1, WINDOW:WINDOW + T, :] = jnp.where(low, zero, tr).astype(BF16)
        kvbuf[base + 2, WINDOW:WINDOW + T, :] = jnp.where(low, tr, zero).astype(BF16)
        kvbuf[base + 3, WINDOW:WINDOW + T, :] = jnp.where(low, zero, t).astype(BF16)

    y_side = {}

    def side_piece(br, o_br, w_ref, c0):
        def run():
            y_side[(br, c0)] = _dot(o_br, w_ref[:, c0:c0 + MERGE_CHUNK])
        return run

    side_list = [side_piece(br, o_br, w_ref, c0) for br, o_br, w_ref in ((0, o_a, wa_ref), (2, o_c, wc_ref))
                 for c0 in range(0, D_MODEL, MERGE_CHUNK)]
    side_units = {n_units - len(side_list) + k: f for k, f in enumerate(side_list)}

    low_w = lax.broadcasted_iota(jnp.int32, (WINDOW, LANES), 1) < HEAD_DIM
    first = jnp.where(i == 0, 1, 0)
    for n in range(nblk):
        bias = bias_ref[first] if n == 0 else bias_ref[0]
        seg = pl.ds(n * WINDOW, 2 * WINDOW)
        for h in range(N_KV_HEADS):
            qs = qbuf[h, n]
            s_lo = _dot_nt(qs, kvbuf[2 * h, seg, :])
            s_hi = _dot_nt(qs, kvbuf[2 * h + 1, seg, :])
            unit = n * N_KV_HEADS + h
            gate_pieces(unit_blocks[unit])
            if unit in side_units:
                side_units[unit]()
            es = {}
            rs = {}
            for half, s_all in ((0, s_lo), (1, s_hi)):
                for j in range(2):
                    head = 4 * h + 2 * j + half
                    s = s_all[j * WINDOW:(j + 1) * WINDOW, :] + bias
                    sink = sinks_ref[head]
                    mx = jnp.maximum(jnp.max(s, axis=-1, keepdims=True), sink)
                    e = jnp.exp(s - mx)
                    den = jnp.sum(e, axis=-1, keepdims=True) + jnp.exp(sink - mx)
                    es[(half, j)] = e.astype(BF16)
                    rs[(half, j)] = 1.0 / den
            p_lo = jnp.concatenate([es[(0, 0)], es[(0, 1)]], axis=0)
            p_hi = jnp.concatenate([es[(1, 0)], es[(1, 1)]], axis=0)
            o = _dot(p_lo, kvbuf[4 + 2 * h, seg, :]) + _dot(p_hi, kvbuf[4 + 2 * h + 1, seg, :])
            for j in range(2):
                oj = o[j * WINDOW:(j + 1) * WINDOW, :]
                col = (2 * h + j) * LANES
                obuf[n * WINDOW:(n + 1) * WINDOW, col:col + LANES] = jnp.where(
                    low_w, oj * rs[(0, j)], oj * rs[(1, j)]).astype(BF16)
    kvbuf[:, 0:WINDOW, :] = kvbuf[:, T:T + WINDOW, :]
    o_b = obuf[...]

    for c0 in range(0, D_MODEL, MERGE_CHUNK):
        cs = slice(c0, c0 + MERGE_CHUNK)
        acc = None
        for br, (o_br, w_ref) in enumerate(((o_a, wa_ref), (o_b, wb_ref), (o_c, wc_ref))):
            g0 = br * D_MODEL + c0
            y_br = y_side[(br, c0)] if (br, c0) in y_side else _dot(o_br, w_ref[:, cs])
            term = gbuf[:, g0:g0 + MERGE_CHUNK] * y_br + y_br
            acc = term if acc is None else acc + term
        mbuf[:, cs] = acc.astype(BF16)
    rows = T // tail_blocks
    for r0 in range(0, T, rows):
        mix = _dot(mbuf[r0:r0 + rows, :], wo_ref[...])
        o_ref[0, r0:r0 + rows, :] = _layer_norm(ALPHA * xf[r0:r0 + rows, :] + mix, g_ref[...], b_ref[...])


def ffn_kernel_buf(x_ref, wup_ref, cw_ref, wdown_ref, g_ref, b_ref, o_ref, cbuf, carry, hbuf, *, T, tail_blocks):
    Z = CONV_HALO
    C = FFN_CHUNK
    i = pl.program_id(1)

    @pl.when(i == 0)
    def _():
        carry[...] = jnp.zeros(carry.shape, F32)

    xf = x_ref[0]
    xb = xf.astype(BF16)
    for c in range(N_FFN_CHUNKS):
        slot = c % 2
        ys = []
        for part in range(2):
            cols = slice(part * D_FF + c * C, part * D_FF + (c + 1) * C)
            here = slice(part * C, (part + 1) * C)
            up = _dot(xb, wup_ref[:, cols])
            cbuf[slot, 0:Z, here] = carry[:, cols]
            cbuf[slot, Z:Z + T, here] = up
            cw = cw_ref[:, cols] * (0.5 if part == 0 else 1.0)
            ys.append(cbuf[slot, Z - 2:Z - 2 + T, here] * cw[0:1, :]
                      + cbuf[slot, Z - 1:Z - 1 + T, here] * cw[1:2, :] + up * cw[2:3, :])
            carry[:, cols] = cbuf[slot, T:T + Z, here]
        half_a, b = ys
        hbuf[:, c * C:(c + 1) * C] = ((half_a + half_a * jnp.tanh(half_a)) * b).astype(BF16)
    rows = T // tail_blocks
    for r0 in range(0, T, rows):
        ffn = _dot(hbuf[r0:r0 + rows, :], wdown_ref[...])
        o_ref[0, r0:r0 + rows, :] = _layer_norm(ALPHA * xf[r0:r0 + rows, :] + ffn, g_ref[...], b_ref[...])


def ffn_kernel(x_ref, wup_ref, cw_ref, wdown_ref, g_ref, b_ref, o_ref, carry, hbuf, *, T, tail_blocks):
    Z = CONV_HALO
    C = FFN_CHUNK
    i = pl.program_id(1)

    @pl.when(i == 0)
    def _():
        carry[...] = jnp.zeros(carry.shape, F32)

    xf = x_ref[0]
    xb = xf.astype(BF16)

    def conv3(up, prev, cw):
        body = pltpu.roll(up, 2, axis=0) * cw[0:1, :] + pltpu.roll(up, 1, axis=0) * cw[1:2, :] + up * cw[2:3, :]
        lead = jnp.concatenate([prev, up[0:2 * Z, :]], axis=0)
        fix = (pltpu.roll(lead, 2, axis=0) * cw[0:1, :] + pltpu.roll(lead, 1, axis=0) * cw[1:2, :]
               + lead * cw[2:3, :])[Z:, :]
        return body, fix

    for c in range(N_FFN_CHUNKS):
        bodies, fixes = [], []
        for part in range(2):
            cols = slice(part * D_FF + c * C, part * D_FF + (c + 1) * C)
            up = _dot(xb, wup_ref[:, cols])
            cw = cw_ref[:, cols] * (0.5 if part == 0 else 1.0)
            body, fix = conv3(up, carry[:, cols], cw)
            carry[:, cols] = up[T - Z:, :]
            bodies.append(body)
            fixes.append(fix)
        act = lambda half_a, b: ((half_a + half_a * jnp.tanh(half_a)) * b).astype(BF16)
        hbuf[:, c * C:(c + 1) * C] = act(*bodies)
        hbuf[0:2 * Z, c * C:(c + 1) * C] = act(*fixes)
    rows = T // tail_blocks
    for r0 in range(0, T, rows):
        ffn = _dot(hbuf[r0:r0 + rows, :], wdown_ref[...])
        o_ref[0, r0:r0 + rows, :] = _layer_norm(ALPHA * xf[r0:r0 + rows, :] + ffn, g_ref[...], b_ref[...])


def _resident(shape, layer):
    nd = len(shape)
    return pl.BlockSpec((None,) + tuple(shape), lambda b, i: (layer,) + (0,) * nd,
                        pipeline_mode=pl.Buffered(1))


def _compiler_params(flags=None):
    return pltpu.CompilerParams(dimension_semantics=("arbitrary", "arbitrary"),
                                vmem_limit_bytes=VMEM_LIMIT_BYTES, flags=flags)


def _rope_call(batch, seq):
    T = ROPE_TILE
    out = pl.BlockSpec((1, T, LANES), lambda b, i: (b, i, 0))
    return pl.pallas_call(
        rope_table_kernel,
        grid=(batch, seq // T),
        in_specs=[pl.BlockSpec((1, 1, T), lambda b, i: (b, 0, i)),
                  pl.BlockSpec((ROT_DIM // 2, T), lambda b, i: (0, 0))],
        out_specs=[out, out],
        out_shape=[jax.ShapeDtypeStruct((batch, seq, LANES), F32)] * 2,
        name="rope_table",
    )


def _mixer_call(layer, batch, seq, T, tail_blocks=1, flags=None):
    tok = lambda width: pl.BlockSpec((1, T, width), lambda b, i: (b, i, 0))
    in_specs = [
        tok(D_MODEL), tok(LANES), tok(LANES),
        pl.BlockSpec((2, WINDOW, 2 * WINDOW), lambda b, i: (0, 0, 0)),
        pl.BlockSpec(memory_space=pltpu.SMEM),
        _resident((D_MODEL, IN_WIDTH), layer),
        _resident((POOL_WIDTH, POOL_WIDTH), layer),
        _resident((1, POOL_WIDTH), layer),
        _resident((3, CONV_WIDTH), layer),
        _resident((POOL_WIDTH, D_MODEL), layer),
        _resident((Q_WIDTH, D_MODEL), layer),
        _resident((CONV_WIDTH, D_MODEL), layer),
        _resident((D_MODEL, D_MODEL), layer),
        _resident((1, D_MODEL), layer),
        _resident((1, D_MODEL), layer),
    ]
    scratch = [
        pltpu.VMEM((POOL_HALO + T, POOL_WIDTH), F32),
        pltpu.VMEM((CONV_HALO + T, CONV_WIDTH), F32),
        pltpu.VMEM((8, WINDOW + T, LANES), BF16),
        pltpu.VMEM((N_KV_HEADS, T // WINDOW, 2 * WINDOW, LANES), BF16),
        pltpu.VMEM((T, Q_WIDTH), BF16),
        pltpu.VMEM((T, D_MODEL), BF16),
        pltpu.VMEM((T, 3 * D_MODEL), F32),
    ]
    return pl.pallas_call(
        functools.partial(mixer_kernel, T=T, tail_blocks=tail_blocks),
        grid=(batch, seq // T),
        in_specs=in_specs,
        out_specs=tok(D_MODEL),
        out_shape=jax.ShapeDtypeStruct((batch, seq, D_MODEL), F32),
        scratch_shapes=scratch,
        compiler_params=_compiler_params(flags),
        name=f"mixer_l{layer}",
    )


def _ffn_call(layer, batch, seq, T, tail_blocks=1, use_roll=True, flags=None):
    tok = pl.BlockSpec((1, T, D_MODEL), lambda b, i: (b, i, 0))
    in_specs = [
        tok,
        _resident((D_MODEL, 2 * D_FF), layer),
        _resident((3, 2 * D_FF), layer),
        _resident((D_FF, D_MODEL), layer),
        _resident((1, D_MODEL), layer),
        _resident((1, D_MODEL), layer),
    ]
    scratch = ([] if use_roll else [pltpu.VMEM((2, CONV_HALO + T, 2 * FFN_CHUNK), F32)]) + [
        pltpu.VMEM((CONV_HALO, 2 * D_FF), F32),
        pltpu.VMEM((T, D_FF), BF16),
    ]
    return pl.pallas_call(
        functools.partial(ffn_kernel if use_roll else ffn_kernel_buf, T=T, tail_blocks=tail_blocks),
        grid=(batch, seq // T),
        in_specs=in_specs,
        out_specs=tok,
        out_shape=jax.ShapeDtypeStruct((batch, seq, D_MODEL), F32),
        scratch_shapes=scratch,
        compiler_params=_compiler_params(flags),
        name=f"ffn_l{layer}",
    )


def _band_bias():
    qi = jnp.arange(WINDOW)[:, None]
    kj = jnp.arange(2 * WINDOW)[None, :]
    band = (kj > qi) & (kj <= qi + WINDOW)
    start = band & (kj >= WINDOW)
    as_bias = lambda valid: jnp.where(valid, 0.0, MASK_VALUE).astype(F32)
    return jnp.stack([as_bias(band), as_bias(start)])


def kernel(x, positions, w_in, w_pool, pool_scale, attn_sinks, conv_w, w_branch_a, w_branch_b,
           w_branch_c, w_o, ln1_g, ln1_b, w_up, ffn_conv_w, w_down, ln2_g, ln2_b):
    batch, seq, _ = x.shape
    inv_freq = ROPE_THETA ** (-jnp.arange(0, ROT_DIM, 2, dtype=F32) / ROT_DIM)
    inv_freq_rows = jnp.broadcast_to(inv_freq[:, None], (ROT_DIM // 2, ROPE_TILE))
    cos_t, sin_t = _rope_call(batch, seq)(positions.reshape(batch, 1, seq), inv_freq_rows)
    bias = _band_bias()

    eye = jnp.eye(N_POOL_GROUPS, dtype=w_pool.dtype)
    wpool_bd = jnp.einsum('lgcd,gh->lgchd', w_pool, eye).reshape(DEPTH, POOL_WIDTH, POOL_WIDTH)

    gate_half = jnp.concatenate([jnp.ones((COL_GATES,), F32), jnp.full((3 * D_MODEL,), 0.5, F32)])
    row = lambda v: v.reshape(DEPTH, 1, v.shape[-1])
    w_in_b = (w_in * gate_half).astype(BF16)
    wpool_b = wpool_bd.astype(BF16)
    wa_b, wb_b, wc_b = w_branch_a.astype(BF16), w_branch_b.astype(BF16), w_branch_c.astype(BF16)
    wo_b = (w_o * 0.5).astype(BF16)
    wup_b = w_up.astype(BF16)
    wdown_b = w_down.astype(BF16)

    for l in range(DEPTH):
        mix_cfg, ffn_cfg = LAYER_VARIANTS[l]
        x = _mixer_call(l, batch, seq, *mix_cfg)(
            x, cos_t, sin_t, bias, attn_sinks[l], w_in_b, wpool_b, row(pool_scale), conv_w,
            wa_b, wb_b, wc_b, wo_b, row(ln1_g), row(ln1_b))
        x = _ffn_call(l, batch, seq, *ffn_cfg)(
            x, wup_b, ffn_conv_w, wdown_b, row(ln2_g), row(ln2_b))
    return x
```

```python
import functools

import jax
import jax.numpy as jnp
from jax import lax
from jax.experimental import pallas as pl
from jax.experimental.pallas import tpu as pltpu

D_MODEL = 1024
DEPTH = 4
N_POOL_GROUPS = 4
POOL_WIDTH = 256
HEAD_DIM = 64
N_Q_HEADS = 8
N_KV_HEADS = 2
Q_WIDTH = 512
KV_WIDTH = 128
WINDOW = 128
ROT_DIM = 16
ROPE_THETA = 500000.0
CONV_WIDTH = 256
D_FF = 2816
ALPHA = (2 * DEPTH) ** 0.25
LN_EPS = 1e-5
MASK_VALUE = -1e30

COL_POOL = 0
COL_Q = COL_POOL + POOL_WIDTH
COL_K = COL_Q + Q_WIDTH
COL_V = COL_K + KV_WIDTH
COL_XC = COL_V + KV_WIDTH
COL_GATES = COL_XC + 3 * CONV_WIDTH
IN_WIDTH = COL_GATES + 3 * D_MODEL

LANES = 128
SUBLANES = 8
MIXER_TILE = 512
FFN_TILE = 512
TAIL_BLOCKS = 2
ROPE_TILE = 2048
POOL_HALO = 16
CONV_HALO = SUBLANES
FFN_CHUNK = 256
N_FFN_CHUNKS = D_FF // FFN_CHUNK
MERGE_CHUNK = 512
GATE_BLOCK = 256
HEAD_GATE_BLOCKS = 4
VMEM_LIMIT_BYTES = 58 * 1024 * 1024

BF16 = jnp.bfloat16
F32 = jnp.float32


def _dot(a, b):
    return jnp.dot(a, b, preferred_element_type=F32)


def _dot_nt(a, b):
    return lax.dot_general(a, b, (((1,), (1,)), ((), ())), preferred_element_type=F32)


def _layer_norm(y, g, b):
    mu = jnp.mean(y, axis=-1, keepdims=True)
    yc = y - mu
    var = jnp.mean(yc * yc, axis=-1, keepdims=True)
    return yc * lax.rsqrt(var + LN_EPS) * g + b


def rope_table_kernel(pos_ref, invf_ref, cos_ref, sin_ref):
    T = ROPE_TILE
    ang = pos_ref[0].astype(F32) * invf_ref[...]
    c8 = jnp.cos(ang)
    s8 = jnp.sin(ang)
    ones = jnp.ones((HEAD_DIM - ROT_DIM, T), F32)
    zeros = jnp.zeros((HEAD_DIM - ROT_DIM, T), F32)
    cos_rows = jnp.concatenate([c8, c8, ones, c8, c8, ones], axis=0)
    sin_rows = jnp.concatenate([-s8, s8, zeros, -s8, s8, zeros], axis=0)
    cos_ref[0] = cos_rows.T
    sin_ref[0] = sin_rows.T


def mixer_kernel(x_ref, cos_ref, sin_ref, bias_ref, sinks_ref, win_ref, wpool_ref,
                 pscale_ref, convw_ref, wa_ref, wb_ref, wc_ref, wo_ref, g_ref, b_ref,
                 o_ref, ubuf, zbuf, kvbuf, qbuf, obuf, mbuf, gbuf, *, T, tail_blocks):
    nblk = T // WINDOW
    i = pl.program_id(1)

    @pl.when(i == 0)
    def _():
        ubuf[0:POOL_HALO, :] = jnp.zeros((POOL_HALO, POOL_WIDTH), F32)
        zbuf[0:CONV_HALO, :] = jnp.zeros((CONV_HALO, CONV_WIDTH), F32)
        kvbuf[:, 0:WINDOW, :] = jnp.zeros((8, WINDOW, LANES), BF16)

    xf = x_ref[0]
    xb = xf.astype(BF16)

    lane = lax.broadcasted_iota(jnp.int32, (T, LANES), 1)
    low = lane < HEAD_DIM

    n_units = nblk * N_KV_HEADS
    gate_blocks = [(COL_GATES + c, c) for c in range(0, 3 * D_MODEL, GATE_BLOCK)]
    rest = gate_blocks[HEAD_GATE_BLOCKS:]
    unit_blocks = [rest[(u * len(rest)) // n_units:((u + 1) * len(rest)) // n_units] for u in range(n_units)]

    def gate_pieces(blocks):
        for src, dst in blocks:
            gbuf[:, dst:dst + GATE_BLOCK] = jnp.tanh(_dot(xb, win_ref[:, src:src + GATE_BLOCK]))

    H = POOL_HALO
    ubuf[H:H + T, :] = _dot(xb, win_ref[:, COL_POOL:COL_POOL + POOL_WIDTH])
    cc = _dot(xb, win_ref[:, COL_XC:COL_XC + 3 * CONV_WIDTH])
    qkv = _dot(xb, win_ref[:, COL_Q:COL_XC])
    gate_pieces(gate_blocks[:HEAD_GATE_BLOCKS])

    u0 = ubuf[H:H + T, 0:LANES]
    u1 = ubuf[H:H + T, LANES:2 * LANES]
    s2 = u0 + ubuf[H - 1:H - 1 + T, 0:LANES]
    s4 = s2 + ubuf[H - 2:H - 2 + T, 0:LANES] + ubuf[H - 3:H - 3 + T, 0:LANES]
    e8 = ubuf[H - 8:H + T, LANES:2 * LANES]
    for j in range(1, 8):
        e8 = e8 + ubuf[H - 8 - j:H + T - j, LANES:2 * LANES]
    s8 = e8[8:, :]
    s16 = s8 + e8[:T, :]
    tpos1 = lax.broadcasted_iota(jnp.int32, (T, LANES), 0) + (i * T + 1)
    cnt_a = jnp.where(low, jnp.minimum(tpos1, 2), jnp.minimum(tpos1, 4)).astype(F32)
    cnt_b = jnp.where(low, jnp.minimum(tpos1, 8), jnp.minimum(tpos1, 16)).astype(F32)
    pooled_a = jnp.where(low, s2, s4) / cnt_a - u0
    pooled_b = jnp.where(low, s8, s16) / cnt_b - u1
    pooled = jnp.concatenate([pooled_a, pooled_b], axis=1).astype(BF16)
    o_a = (_dot(pooled, wpool_ref[...]) * pscale_ref[...]).astype(BF16)
    ubuf[0:H, :] = ubuf[T:T + H, :]

    Z = CONV_HALO
    xc = cc[:, 0:CONV_WIDTH]
    gate_b = cc[:, CONV_WIDTH:2 * CONV_WIDTH]
    gate_c = cc[:, 2 * CONV_WIDTH:3 * CONV_WIDTH]
    z = gate_c * xc
    zbuf[Z:Z + T, :] = z
    cw = convw_ref[...]
    conv = zbuf[Z - 2:Z - 2 + T, :] * cw[0:1, :] + zbuf[Z - 1:Z - 1 + T, :] * cw[1:2, :] + z * cw[2:3, :]
    o_c = (gate_b * conv).astype(BF16)
    zbuf[0:Z, :] = zbuf[T:T + Z, :]

    cosv = cos_ref[0]
    sinv = sin_ref[0]
    rot_low = (lane & (HEAD_DIM - 1)) < (ROT_DIM // 2)

    def rope(t):
        partner = jnp.where(rot_low, pltpu.roll(t, LANES - ROT_DIM // 2, axis=1),
                            pltpu.roll(t, ROT_DIM // 2, axis=1))
        return t * cosv + partner * sinv

    for m in range(N_Q_HEADS // 2):
        qm = (rope(qkv[:, m * LANES:(m + 1) * LANES]) * (HEAD_DIM ** -0.5)).astype(BF16)
        for n in range(nblk):
            qbuf[m // 2, n, (m % 2) * WINDOW:(m % 2 + 1) * WINDOW, :] = qm[n * WINDOW:(n + 1) * WINDOW, :]

    kr = rope(qkv[:, Q_WIDTH:Q_WIDTH + KV_WIDTH])
    vv = qkv[:, Q_WIDTH + KV_WIDTH:Q_WIDTH + 2 * KV_WIDTH]
    zero = jnp.zeros((T, LANES), F32)
    for base, t in ((0, kr), (4, vv)):
        tr = pltpu.roll(t, HEAD_DIM, axis=1)
        kvbuf[base + 0, WINDOW:WINDOW + T, :] = jnp.where(low, t, zero).astype(BF16)
        kvbuf[base + 1, WINDOW:WINDOW + T, :] = jnp.where(low, zero, tr).astype(BF16)
        kvbuf[base + 2, WINDOW:WINDOW + T, :] = jnp.where(low, tr, zero).astype(BF16)
        kvbuf[base + 3, WINDOW:WINDOW + T, :] = jnp.where(low, zero, t).astype(BF16)

    y_side = {}

    def side_piece(br, o_br, w_ref, c0):
        def run():
            y_side[(br, c0)] = _dot(o_br, w_ref[:, c0:c0 + MERGE_CHUNK])
        return run

    side_list = [side_piece(br, o_br, w_ref, c0) for br, o_br, w_ref in ((0, o_a, wa_ref), (2, o_c, wc_ref))
                 for c0 in range(0, D_MODEL, MERGE_CHUNK)]
    side_units = {n_units - len(side_list) + k: f for k, f in enumerate(side_list)}

    low_w = lax.broadcasted_iota(jnp.int32, (WINDOW, LANES), 1) < HEAD_DIM
    first = jnp.where(i == 0, 1, 0)
    for n in range(nblk):
        bias = bias_ref[first] if n == 0 else bias_ref[0]
        seg = pl.ds(n * WINDOW, 2 * WINDOW)
        for h in range(N_KV_HEADS):
            qs = qbuf[h, n]
            s_lo = _dot_nt(qs, kvbuf[2 * h, seg, :])
            s_hi = _dot_nt(qs, kvbuf[2 * h + 1, seg, :])
            unit = n * N_KV_HEADS + h
            gate_pieces(unit_blocks[unit])
            if unit in side_units:
                side_units[unit]()
            es = {}
            rs = {}
            for half, s_all in ((0, s_lo), (1, s_hi)):
                for j in range(2):
                    head = 4 * h + 2 * j + half
                    s = s_all[j * WINDOW:(j + 1) * WINDOW, :] + bias
                    sink = sinks_ref[head]
                    mx = jnp.maximum(jnp.max(s, axis=-1, keepdims=True), sink)
                    e = jnp.exp(s - mx)
                    den = jnp.sum(e, axis=-1, keepdims=True) + jnp.exp(sink - mx)
                    es[(half, j)] = e.astype(BF16)
                    rs[(half, j)] = 1.0 / den
            p_lo = jnp.concatenate([es[(0, 0)], es[(0, 1)]], axis=0)
            p_hi = jnp.concatenate([es[(1, 0)], es[(1, 1)]], axis=0)
            o = _dot(p_lo, kvbuf[4 + 2 * h, seg, :]) + _dot(p_hi, kvbuf[4 + 2 * h + 1, seg, :])
            for j in range(2):
                oj = o[j * WINDOW:(j + 1) * WINDOW, :]
                col = (2 * h + j) * LANES
                obuf[n * WINDOW:(n + 1) * WINDOW, col:col + LANES] = jnp.where(
                    low_w, oj * rs[(0, j)], oj * rs[(1, j)]).astype(BF16)
    kvbuf[:, 0:WINDOW, :] = kvbuf[:, T:T + WINDOW, :]
    o_b = obuf[...]

    for c0 in range(0, D_MODEL, MERGE_CHUNK):
        cs = slice(c0, c0 + MERGE_CHUNK)
        acc = None
        for br, (o_br, w_ref) in enumerate(((o_a, wa_ref), (o_b, wb_ref), (o_c, wc_ref))):
            g0 = br * D_MODEL + c0
            y_br = y_side[(br, c0)] if (br, c0) in y_side else _dot(o_br, w_ref[:, cs])
            term = gbuf[:, g0:g0 + MERGE_CHUNK] * y_br + y_br
            acc = term if acc is None else acc + term
        mbuf[:, cs] = acc.astype(BF16)
    rows = T // tail_blocks
    for r0 in range(0, T, rows):
        mix = _dot(mbuf[r0:r0 + rows, :], wo_ref[...])
        o_ref[0, r0:r0 + rows, :] = _layer_norm(ALPHA * xf[r0:r0 + rows, :] + mix, g_ref[...], b_ref[...])


def ffn_kernel(x_ref, wup_ref, cw_ref, wdown_ref, g_ref, b_ref, o_ref, carry, hbuf, *, T, tail_blocks):
    Z = CONV_HALO
    C = FFN_CHUNK
    i = pl.program_id(1)

    @pl.when(i == 0)
    def _():
        carry[...] = jnp.zeros(carry.shape, F32)

    xf = x_ref[0]
    xb = xf.astype(BF16)

    def conv3(up, prev, cw):
        body = pltpu.roll(up, 2, axis=0) * cw[0:1, :] + pltpu.roll(up, 1, axis=0) * cw[1:2, :] + up * cw[2:3, :]
        lead = jnp.concatenate([prev, up[0:2 * Z, :]], axis=0)
        fix = (pltpu.roll(lead, 2, axis=0) * cw[0:1, :] + pltpu.roll(lead, 1, axis=0) * cw[1:2, :]
               + lead * cw[2:3, :])[Z:, :]
        return body, fix

    for c in range(N_FFN_CHUNKS):
        bodies, fixes = [], []
        for part in range(2):
            cols = slice(part * D_FF + c * C, part * D_FF + (c + 1) * C)
            up = _dot(xb, wup_ref[:, cols])
            cw = cw_ref[:, cols] * (0.5 if part == 0 else 1.0)
            body, fix = conv3(up, carry[:, cols], cw)
            carry[:, cols] = up[T - Z:, :]
            bodies.append(body)
            fixes.append(fix)
        act = lambda half_a, b: ((half_a + half_a * jnp.tanh(half_a)) * b).astype(BF16)
        hbuf[:, c * C:(c + 1) * C] = act(*bodies)
        hbuf[0:2 * Z, c * C:(c + 1) * C] = act(*fixes)
    rows = T // tail_blocks
    for r0 in range(0, T, rows):
        ffn = _dot(hbuf[r0:r0 + rows, :], wdown_ref[...])
        o_ref[0, r0:r0 + rows, :] = _layer_norm(ALPHA * xf[r0:r0 + rows, :] + ffn, g_ref[...], b_ref[...])


def _resident(shape, layer):
    nd = len(shape)
    return pl.BlockSpec((None,) + tuple(shape), lambda b, i: (layer,) + (0,) * nd,
                        pipeline_mode=pl.Buffered(1))


def _compiler_params():
    return pltpu.CompilerParams(dimension_semantics=("arbitrary", "arbitrary"),
                                vmem_limit_bytes=VMEM_LIMIT_BYTES)


def _rope_call(batch, seq):
    T = ROPE_TILE
    out = pl.BlockSpec((1, T, LANES), lambda b, i: (b, i, 0))
    return pl.pallas_call(
        rope_table_kernel,
        grid=(batch, seq // T),
        in_specs=[pl.BlockSpec((1, 1, T), lambda b, i: (b, 0, i)),
                  pl.BlockSpec((ROT_DIM // 2, T), lambda b, i: (0, 0))],
        out_specs=[out, out],
        out_shape=[jax.ShapeDtypeStruct((batch, seq, LANES), F32)] * 2,
        name="rope_table",
    )


def _mixer_call(layer, batch, seq, T, tail_blocks):
    tok = lambda width: pl.BlockSpec((1, T, width), lambda b, i: (b, i, 0))
    in_specs = [
        tok(D_MODEL), tok(LANES), tok(LANES),
        pl.BlockSpec((2, WINDOW, 2 * WINDOW), lambda b, i: (0, 0, 0)),
        pl.BlockSpec(memory_space=pltpu.SMEM),
        _resident((D_MODEL, IN_WIDTH), layer),
        _resident((POOL_WIDTH, POOL_WIDTH), layer),
        _resident((1, POOL_WIDTH), layer),
        _resident((3, CONV_WIDTH), layer),
        _resident((POOL_WIDTH, D_MODEL), layer),
        _resident((Q_WIDTH, D_MODEL), layer),
        _resident((CONV_WIDTH, D_MODEL), layer),
        _resident((D_MODEL, D_MODEL), layer),
        _resident((1, D_MODEL), layer),
        _resident((1, D_MODEL), layer),
    ]
    scratch = [
        pltpu.VMEM((POOL_HALO + T, POOL_WIDTH), F32),
        pltpu.VMEM((CONV_HALO + T, CONV_WIDTH), F32),
        pltpu.VMEM((8, WINDOW + T, LANES), BF16),
        pltpu.VMEM((N_KV_HEADS, T // WINDOW, 2 * WINDOW, LANES), BF16),
        pltpu.VMEM((T, Q_WIDTH), BF16),
        pltpu.VMEM((T, D_MODEL), BF16),
        pltpu.VMEM((T, 3 * D_MODEL), F32),
    ]
    return pl.pallas_call(
        functools.partial(mixer_kernel, T=T, tail_blocks=tail_blocks),
        grid=(batch, seq // T),
        in_specs=in_specs,
        out_specs=tok(D_MODEL),
        out_shape=jax.ShapeDtypeStruct((batch, seq, D_MODEL), F32),
        scratch_shapes=scratch,
        compiler_params=_compiler_params(),
        name=f"mixer_l{layer}",
    )


def _ffn_call(layer, batch, seq, T, tail_blocks):
    tok = pl.BlockSpec((1, T, D_MODEL), lambda b, i: (b, i, 0))
    in_specs = [
        tok,
        _resident((D_MODEL, 2 * D_FF), layer),
        _resident((3, 2 * D_FF), layer),
        _resident((D_FF, D_MODEL), layer),
        _resident((1, D_MODEL), layer),
        _resident((1, D_MODEL), layer),
    ]
    scratch = [
        pltpu.VMEM((CONV_HALO, 2 * D_FF), F32),
        pltpu.VMEM((T, D_FF), BF16),
    ]
    return pl.pallas_call(
        functools.partial(ffn_kernel, T=T, tail_blocks=tail_blocks),
        grid=(batch, seq // T),
        in_specs=in_specs,
        out_specs=tok,
        out_shape=jax.ShapeDtypeStruct((batch, seq, D_MODEL), F32),
        scratch_shapes=scratch,
        compiler_params=_compiler_params(),
        name=f"ffn_l{layer}",
    )


def _band_bias():
    qi = jnp.arange(WINDOW)[:, None]
    kj = jnp.arange(2 * WINDOW)[None, :]
    band = (kj > qi) & (kj <= qi + WINDOW)
    start = band & (kj >= WINDOW)
    as_bias = lambda valid: jnp.where(valid, 0.0, MASK_VALUE).astype(F32)
    return jnp.stack([as_bias(band), as_bias(start)])


def kernel(x, positions, w_in, w_pool, pool_scale, attn_sinks, conv_w, w_branch_a, w_branch_b,
           w_branch_c, w_o, ln1_g, ln1_b, w_up, ffn_conv_w, w_down, ln2_g, ln2_b):
    batch, seq, _ = x.shape
    assert seq % MIXER_TILE == 0 and seq % FFN_TILE == 0 and seq % ROPE_TILE == 0 and MIXER_TILE % WINDOW == 0
    inv_freq = ROPE_THETA ** (-jnp.arange(0, ROT_DIM, 2, dtype=F32) / ROT_DIM)
    inv_freq_rows = jnp.broadcast_to(inv_freq[:, None], (ROT_DIM // 2, ROPE_TILE))
    cos_t, sin_t = _rope_call(batch, seq)(positions.reshape(batch, 1, seq), inv_freq_rows)
    bias = _band_bias()

    eye = jnp.eye(N_POOL_GROUPS, dtype=w_pool.dtype)
    wpool_bd = jnp.einsum('lgcd,gh->lgchd', w_pool, eye).reshape(DEPTH, POOL_WIDTH, POOL_WIDTH)

    gate_half = jnp.concatenate([jnp.ones((COL_GATES,), F32), jnp.full((3 * D_MODEL,), 0.5, F32)])
    row = lambda v: v.reshape(DEPTH, 1, v.shape[-1])
    w_in_b = (w_in * gate_half).astype(BF16)
    wpool_b = wpool_bd.astype(BF16)
    wa_b, wb_b, wc_b = w_branch_a.astype(BF16), w_branch_b.astype(BF16), w_branch_c.astype(BF16)
    wo_b = (w_o * 0.5).astype(BF16)
    wup_b = w_up.astype(BF16)
    wdown_b = w_down.astype(BF16)

    for l in range(DEPTH):
        x = _mixer_call(l, batch, seq, MIXER_TILE, TAIL_BLOCKS)(
            x, cos_t, sin_t, bias, attn_sinks[l], w_in_b, wpool_b, row(pool_scale), conv_w,
            wa_b, wb_b, wc_b, wo_b, row(ln1_g), row(ln1_b))
        x = _ffn_call(l, batch, seq, FFN_TILE, TAIL_BLOCKS)(
            x, wup_b, ffn_conv_w, wdown_b, row(ln2_g), row(ln2_b))
    return x
```

```python
import functools

import jax
import jax.numpy as jnp
from jax import lax
from jax.experimental import pallas as pl
from jax.experimental.pallas import tpu as pltpu

D_MODEL = 1024
DEPTH = 4
N_POOL_GROUPS = 4
POOL_WIDTH = 256
HEAD_DIM = 64
N_Q_HEADS = 8
N_KV_HEADS = 2
Q_WIDTH = 512
KV_WIDTH = 128
WINDOW = 128
ROT_DIM = 16
ROPE_THETA = 500000.0
CONV_WIDTH = 256
D_FF = 2816
ALPHA = (2 * DEPTH) ** 0.25
LN_EPS = 1e-5
MASK_VALUE = -1e30

COL_POOL = 0
COL_Q = COL_POOL + POOL_WIDTH
COL_K = COL_Q + Q_WIDTH
COL_V = COL_K + KV_WIDTH
COL_XC = COL_V + KV_WIDTH
COL_GATES = COL_XC + 3 * CONV_WIDTH
IN_WIDTH = COL_GATES + 3 * D_MODEL

LANES = 128
SUBLANES = 8
MIXER_TILE = 512
FFN_TILE = 512
FFN_VARIANTS = ((512, 2, False), (512, 4, False), (1024, 4, False), (512, 2, True))
TAIL_BLOCKS = 2
ROPE_TILE = 2048
POOL_HALO = 16
CONV_HALO = SUBLANES
FFN_CHUNK = 256
N_FFN_CHUNKS = D_FF // FFN_CHUNK
MERGE_CHUNK = 512
GATE_BLOCK = 256
HEAD_GATE_BLOCKS = 4
VMEM_LIMIT_BYTES = 58 * 1024 * 1024

BF16 = jnp.bfloat16
F32 = jnp.float32


def _dot(a, b):
    return jnp.dot(a, b, preferred_element_type=F32)


def _dot_nt(a, b):
    return lax.dot_general(a, b, (((1,), (1,)), ((), ())), preferred_element_type=F32)


def _layer_norm(y, g, b):
    mu = jnp.mean(y, axis=-1, keepdims=True)
    yc = y - mu
    var = jnp.mean(yc * yc, axis=-1, keepdims=True)
    return yc * lax.rsqrt(var + LN_EPS) * g + b


def rope_table_kernel(pos_ref, invf_ref, cos_ref, sin_ref):
    T = ROPE_TILE
    ang = pos_ref[0].astype(F32) * invf_ref[...]
    c8 = jnp.cos(ang)
    s8 = jnp.sin(ang)
    ones = jnp.ones((HEAD_DIM - ROT_DIM, T), F32)
    zeros = jnp.zeros((HEAD_DIM - ROT_DIM, T), F32)
    cos_rows = jnp.concatenate([c8, c8, ones, c8, c8, ones], axis=0)
    sin_rows = jnp.concatenate([-s8, s8, zeros, -s8, s8, zeros], axis=0)
    cos_ref[0] = cos_rows.T
    sin_ref[0] = sin_rows.T


def mixer_kernel(x_ref, cos_ref, sin_ref, bias_ref, sinks_ref, win_ref, wpool_ref,
                 pscale_ref, convw_ref, wa_ref, wb_ref, wc_ref, wo_ref, g_ref, b_ref,
                 o_ref, ubuf, zbuf, kvbuf, qbuf, obuf, mbuf, gbuf, *, T, tail_blocks):
    nblk = T // WINDOW
    i = pl.program_id(1)

    @pl.when(i == 0)
    def _():
        ubuf[0:POOL_HALO, :] = jnp.zeros((POOL_HALO, POOL_WIDTH), F32)
        zbuf[0:CONV_HALO, :] = jnp.zeros((CONV_HALO, CONV_WIDTH), F32)
        kvbuf[:, 0:WINDOW, :] = jnp.zeros((8, WINDOW, LANES), BF16)

    xf = x_ref[0]
    xb = xf.astype(BF16)

    lane = lax.broadcasted_iota(jnp.int32, (T, LANES), 1)
    low = lane < HEAD_DIM

    n_units = nblk * N_KV_HEADS
    gate_blocks = [(COL_GATES + c, c) for c in range(0, 3 * D_MODEL, GATE_BLOCK)]
    rest = gate_blocks[HEAD_GATE_BLOCKS:]
    unit_blocks = [rest[(u * len(rest)) // n_units:((u + 1) * len(rest)) // n_units] for u in range(n_units)]

    def gate_pieces(blocks):
        for src, dst in blocks:
            gbuf[:, dst:dst + GATE_BLOCK] = jnp.tanh(_dot(xb, win_ref[:, src:src + GATE_BLOCK]))

    H = POOL_HALO
    ubuf[H:H + T, :] = _dot(xb, win_ref[:, COL_POOL:COL_POOL + POOL_WIDTH])
    cc = _dot(xb, win_ref[:, COL_XC:COL_XC + 3 * CONV_WIDTH])
    qkv = _dot(xb, win_ref[:, COL_Q:COL_XC])
    gate_pieces(gate_blocks[:HEAD_GATE_BLOCKS])

    u0 = ubuf[H:H + T, 0:LANES]
    u1 = ubuf[H:H + T, LANES:2 * LANES]
    s2 = u0 + ubuf[H - 1:H - 1 + T, 0:LANES]
    s4 = s2 + ubuf[H - 2:H - 2 + T, 0:LANES] + ubuf[H - 3:H - 3 + T, 0:LANES]
    e8 = ubuf[H - 8:H + T, LANES:2 * LANES]
    for j in range(1, 8):
        e8 = e8 + ubuf[H - 8 - j:H + T - j, LANES:2 * LANES]
    s8 = e8[8:, :]
    s16 = s8 + e8[:T, :]
    tpos1 = lax.broadcasted_iota(jnp.int32, (T, LANES), 0) + (i * T + 1)
    cnt_a = jnp.where(low, jnp.minimum(tpos1, 2), jnp.minimum(tpos1, 4)).astype(F32)
    cnt_b = jnp.where(low, jnp.minimum(tpos1, 8), jnp.minimum(tpos1, 16)).astype(F32)
    pooled_a = jnp.where(low, s2, s4) / cnt_a - u0
    pooled_b = jnp.where(low, s8, s16) / cnt_b - u1
    pooled = jnp.concatenate([pooled_a, pooled_b], axis=1).astype(BF16)
    o_a = (_dot(pooled, wpool_ref[...]) * pscale_ref[...]).astype(BF16)
    ubuf[0:H, :] = ubuf[T:T + H, :]

    Z = CONV_HALO
    xc = cc[:, 0:CONV_WIDTH]
    gate_b = cc[:, CONV_WIDTH:2 * CONV_WIDTH]
    gate_c = cc[:, 2 * CONV_WIDTH:3 * CONV_WIDTH]
    z = gate_c * xc
    zbuf[Z:Z + T, :] = z
    cw = convw_ref[...]
    conv = zbuf[Z - 2:Z - 2 + T, :] * cw[0:1, :] + zbuf[Z - 1:Z - 1 + T, :] * cw[1:2, :] + z * cw[2:3, :]
    o_c = (gate_b * conv).astype(BF16)
    zbuf[0:Z, :] = zbuf[T:T + Z, :]

    cosv = cos_ref[0]
    sinv = sin_ref[0]
    rot_low = (lane & (HEAD_DIM - 1)) < (ROT_DIM // 2)

    def rope(t):
        partner = jnp.where(rot_low, pltpu.roll(t, LANES - ROT_DIM // 2, axis=1),
                            pltpu.roll(t, ROT_DIM // 2, axis=1))
        return t * cosv + partner * sinv

    for m in range(N_Q_HEADS // 2):
        qm = (rope(qkv[:, m * LANES:(m + 1) * LANES]) * (HEAD_DIM ** -0.5)).astype(BF16)
        for n in range(nblk):
            qbuf[m // 2, n, (m % 2) * WINDOW:(m % 2 + 1) * WINDOW, :] = qm[n * WINDOW:(n + 1) * WINDOW, :]

    kr = rope(qkv[:, Q_WIDTH:Q_WIDTH + KV_WIDTH])
    vv = qkv[:, Q_WIDTH + KV_WIDTH:Q_WIDTH + 2 * KV_WIDTH]
    zero = jnp.zeros((T, LANES), F32)
    for base, t in ((0, kr), (4, vv)):
        tr = pltpu.roll(t, HEAD_DIM, axis=1)
        kvbuf[base + 0, WINDOW:WINDOW + T, :] = jnp.where(low, t, zero).astype(BF16)
        kvbuf[base + 1, WINDOW:WINDOW + T, :] = jnp.where(low, zero, tr).astype(BF16)
        kvbuf[base + 2, WINDOW:WINDOW + T, :] = jnp.where(low, tr, zero).astype(BF16)
        kvbuf[base + 3, WINDOW:WINDOW + T, :] = jnp.where(low, zero, t).astype(BF16)

    y_side = {}

    def side_piece(br, o_br, w_ref, c0):
        def run():
            y_side[(br, c0)] = _dot(o_br, w_ref[:, c0:c0 + MERGE_CHUNK])
        return run

    side_list = [side_piece(br, o_br, w_ref, c0) for br, o_br, w_ref in ((0, o_a, wa_ref), (2, o_c, wc_ref))
                 for c0 in range(0, D_MODEL, MERGE_CHUNK)]
    side_units = {n_units - len(side_list) + k: f for k, f in enumerate(side_list)}

    low_w = lax.broadcasted_iota(jnp.int32, (WINDOW, LANES), 1) < HEAD_DIM
    first = jnp.where(i == 0, 1, 0)
    for n in range(nblk):
        bias = bias_ref[first] if n == 0 else bias_ref[0]
        seg = pl.ds(n * WINDOW, 2 * WINDOW)
        for h in range(N_KV_HEADS):
            qs = qbuf[h, n]
            s_lo = _dot_nt(qs, kvbuf[2 * h, seg, :])
            s_hi = _dot_nt(qs, kvbuf[2 * h + 1, seg, :])
            unit = n * N_KV_HEADS + h
            gate_pieces(unit_blocks[unit])
            if unit in side_units:
                side_units[unit]()
            es = {}
            rs = {}
            for half, s_all in ((0, s_lo), (1, s_hi)):
                for j in range(2):
                    head = 4 * h + 2 * j + half
                    s = s_all[j * WINDOW:(j + 1) * WINDOW, :] + bias
                    sink = sinks_ref[head]
                    mx = jnp.maximum(jnp.max(s, axis=-1, keepdims=True), sink)
                    e = jnp.exp(s - mx)
                    den = jnp.sum(e, axis=-1, keepdims=True) + jnp.exp(sink - mx)
                    es[(half, j)] = e.astype(BF16)
                    rs[(half, j)] = 1.0 / den
            p_lo = jnp.concatenate([es[(0, 0)], es[(0, 1)]], axis=0)
            p_hi = jnp.concatenate([es[(1, 0)], es[(1, 1)]], axis=0)
            o = _dot(p_lo, kvbuf[4 + 2 * h, seg, :]) + _dot(p_hi, kvbuf[4 + 2 * h + 1, seg, :])
            for j in range(2):
                oj = o[j * WINDOW:(j + 1) * WINDOW, :]
                col = (2 * h + j) * LANES
                obuf[n * WINDOW:(n + 1) * WINDOW, col:col + LANES] = jnp.where(
                    low_w, oj * rs[(0, j)], oj * rs[(1, j)]).astype(BF16)
    kvbuf[:, 0:WINDOW, :] = kvbuf[:, T:T + WINDOW, :]
    o_b = obuf[...]

    for c0 in range(0, D_MODEL, MERGE_CHUNK):
        cs = slice(c0, c0 + MERGE_CHUNK)
        acc = None
        for br, (o_br, w_ref) in enumerate(((o_a, wa_ref), (o_b, wb_ref), (o_c, wc_ref))):
            g0 = br * D_MODEL + c0
            y_br = y_side[(br, c0)] if (br, c0) in y_side else _dot(o_br, w_ref[:, cs])
            term = gbuf[:, g0:g0 + MERGE_CHUNK] * y_br + y_br
            acc = term if acc is None else acc + term
        mbuf[:, cs] = acc.astype(BF16)
    rows = T // tail_blocks
    for r0 in range(0, T, rows):
        mix = _dot(mbuf[r0:r0 + rows, :], wo_ref[...])
        o_ref[0, r0:r0 + rows, :] = _layer_norm(ALPHA * xf[r0:r0 + rows, :] + mix, g_ref[...], b_ref[...])


def ffn_kernel(x_ref, wup_ref, cw_ref, wdown_ref, g_ref, b_ref, o_ref, carry, hbuf, *, T, tail_blocks, wide):
    Z = CONV_HALO
    C = FFN_CHUNK
    i = pl.program_id(1)

    @pl.when(i == 0)
    def _():
        carry[...] = jnp.zeros(carry.shape, F32)

    xf = x_ref[0]
    xb = xf.astype(BF16)

    def conv3(up, prev, cw):
        body = pltpu.roll(up, 2, axis=0) * cw[0:1, :] + pltpu.roll(up, 1, axis=0) * cw[1:2, :] + up * cw[2:3, :]
        lead = jnp.concatenate([prev, up[0:2 * Z, :]], axis=0)
        fix = (pltpu.roll(lead, 2, axis=0) * cw[0:1, :] + pltpu.roll(lead, 1, axis=0) * cw[1:2, :]
               + lead * cw[2:3, :])[Z:, :]
        return body, fix

    starts = (list(range(0, D_FF - 256, 512)) + [D_FF - 256]) if wide else list(range(0, D_FF, FFN_CHUNK))
    for c0 in starts:
        C = min(512 if wide else FFN_CHUNK, D_FF - c0)
        bodies, fixes = [], []
        for part in range(2):
            cols = slice(part * D_FF + c0, part * D_FF + c0 + C)
            up = _dot(xb, wup_ref[:, cols])
            cw = cw_ref[:, cols] * (0.5 if part == 0 else 1.0)
            body, fix = conv3(up, carry[:, cols], cw)
            carry[:, cols] = up[T - Z:, :]
            bodies.append(body)
            fixes.append(fix)
        act = lambda half_a, b: ((half_a + half_a * jnp.tanh(half_a)) * b).astype(BF16)
        hbuf[:, c0:c0 + C] = act(*bodies)
        hbuf[0:2 * Z, c0:c0 + C] = act(*fixes)
    rows = T // tail_blocks
    for r0 in range(0, T, rows):
        ffn = _dot(hbuf[r0:r0 + rows, :], wdown_ref[...])
        o_ref[0, r0:r0 + rows, :] = _layer_norm(ALPHA * xf[r0:r0 + rows, :] + ffn, g_ref[...], b_ref[...])


def _resident(shape, layer):
    nd = len(shape)
    return pl.BlockSpec((None,) + tuple(shape), lambda b, i: (layer,) + (0,) * nd,
                        pipeline_mode=pl.Buffered(1))


def _compiler_params():
    return pltpu.CompilerParams(dimension_semantics=("arbitrary", "arbitrary"),
                                vmem_limit_bytes=VMEM_LIMIT_BYTES)


def _rope_call(batch, seq):
    T = ROPE_TILE
    out = pl.BlockSpec((1, T, LANES), lambda b, i: (b, i, 0))
    return pl.pallas_call(
        rope_table_kernel,
        grid=(batch, seq // T),
        in_specs=[pl.BlockSpec((1, 1, T), lambda b, i: (b, 0, i)),
                  pl.BlockSpec((ROT_DIM // 2, T), lambda b, i: (0, 0))],
        out_specs=[out, out],
        out_shape=[jax.ShapeDtypeStruct((batch, seq, LANES), F32)] * 2,
        name="rope_table",
    )


def _mixer_call(layer, batch, seq, T, tail_blocks):
    tok = lambda width: pl.BlockSpec((1, T, width), lambda b, i: (b, i, 0))
    in_specs = [
        tok(D_MODEL), tok(LANES), tok(LANES),
        pl.BlockSpec((2, WINDOW, 2 * WINDOW), lambda b, i: (0, 0, 0)),
        pl.BlockSpec(memory_space=pltpu.SMEM),
        _resident((D_MODEL, IN_WIDTH), layer),
        _resident((POOL_WIDTH, POOL_WIDTH), layer),
        _resident((1, POOL_WIDTH), layer),
        _resident((3, CONV_WIDTH), layer),
        _resident((POOL_WIDTH, D_MODEL), layer),
        _resident((Q_WIDTH, D_MODEL), layer),
        _resident((CONV_WIDTH, D_MODEL), layer),
        _resident((D_MODEL, D_MODEL), layer),
        _resident((1, D_MODEL), layer),
        _resident((1, D_MODEL), layer),
    ]
    scratch = [
        pltpu.VMEM((POOL_HALO + T, POOL_WIDTH), F32),
        pltpu.VMEM((CONV_HALO + T, CONV_WIDTH), F32),
        pltpu.VMEM((8, WINDOW + T, LANES), BF16),
        pltpu.VMEM((N_KV_HEADS, T // WINDOW, 2 * WINDOW, LANES), BF16),
        pltpu.VMEM((T, Q_WIDTH), BF16),
        pltpu.VMEM((T, D_MODEL), BF16),
        pltpu.VMEM((T, 3 * D_MODEL), F32),
    ]
    return pl.pallas_call(
        functools.partial(mixer_kernel, T=T, tail_blocks=tail_blocks),
        grid=(batch, seq // T),
        in_specs=in_specs,
        out_specs=tok(D_MODEL),
        out_shape=jax.ShapeDtypeStruct((batch, seq, D_MODEL), F32),
        scratch_shapes=scratch,
        compiler_params=_compiler_params(),
        name=f"mixer_l{layer}",
    )


def _ffn_call(layer, batch, seq, T, tail_blocks, wide=False):
    tok = pl.BlockSpec((1, T, D_MODEL), lambda b, i: (b, i, 0))
    in_specs = [
        tok,
        _resident((D_MODEL, 2 * D_FF), layer),
        _resident((3, 2 * D_FF), layer),
        _resident((D_FF, D_MODEL), layer),
        _resident((1, D_MODEL), layer),
        _resident((1, D_MODEL), layer),
    ]
    scratch = [
        pltpu.VMEM((CONV_HALO, 2 * D_FF), F32),
        pltpu.VMEM((T, D_FF), BF16),
    ]
    return pl.pallas_call(
        functools.partial(ffn_kernel, T=T, tail_blocks=tail_blocks, wide=wide),
        grid=(batch, seq // T),
        in_specs=in_specs,
        out_specs=tok,
        out_shape=jax.ShapeDtypeStruct((batch, seq, D_MODEL), F32),
        scratch_shapes=scratch,
        compiler_params=_compiler_params(),
        name=f"ffn_l{layer}",
    )


def _band_bias():
    qi = jnp.arange(WINDOW)[:, None]
    kj = jnp.arange(2 * WINDOW)[None, :]
    band = (kj > qi) & (kj <= qi + WINDOW)
    start = band & (kj >= WINDOW)
    as_bias = lambda valid: jnp.where(valid, 0.0, MASK_VALUE).astype(F32)
    return jnp.stack([as_bias(band), as_bias(start)])


def kernel(x, positions, w_in, w_pool, pool_scale, attn_sinks, conv_w, w_branch_a, w_branch_b,
           w_branch_c, w_o, ln1_g, ln1_b, w_up, ffn_conv_w, w_down, ln2_g, ln2_b):
    batch, seq, _ = x.shape
    assert seq % MIXER_TILE == 0 and seq % FFN_TILE == 0 and seq % ROPE_TILE == 0 and MIXER_TILE % WINDOW == 0
    inv_freq = ROPE_THETA ** (-jnp.arange(0, ROT_DIM, 2, dtype=F32) / ROT_DIM)
    inv_freq_rows = jnp.broadcast_to(inv_freq[:, None], (ROT_DIM // 2, ROPE_TILE))
    cos_t, sin_t = _rope_call(batch, seq)(positions.reshape(batch, 1, seq), inv_freq_rows)
    bias = _band_bias()

    eye = jnp.eye(N_POOL_GROUPS, dtype=w_pool.dtype)
    wpool_bd = jnp.einsum('lgcd,gh->lgchd', w_pool, eye).reshape(DEPTH, POOL_WIDTH, POOL_WIDTH)

    gate_half = jnp.concatenate([jnp.ones((COL_GATES,), F32), jnp.full((3 * D_MODEL,), 0.5, F32)])
    row = lambda v: v.reshape(DEPTH, 1, v.shape[-1])
    w_in_b = (w_in * gate_half).astype(BF16)
    wpool_b = wpool_bd.astype(BF16)
    wa_b, wb_b, wc_b = w_branch_a.astype(BF16), w_branch_b.astype(BF16), w_branch_c.astype(BF16)
    wo_b = (w_o * 0.5).astype(BF16)
    wup_b = w_up.astype(BF16)
    wdown_b = w_down.astype(BF16)

    for l in range(DEPTH):
        x = _mixer_call(l, batch, seq, MIXER_TILE, TAIL_BLOCKS)(
            x, cos_t, sin_t, bias, attn_sinks[l], w_in_b, wpool_b, row(pool_scale), conv_w,
            wa_b, wb_b, wc_b, wo_b, row(ln1_g), row(ln1_b))
        x = _ffn_call(l, batch, seq, *FFN_VARIANTS[l])(
            x, wup_b, ffn_conv_w, wdown_b, row(ln2_g), row(ln2_b))
    return x
```

```python
import functools

import jax
import jax.numpy as jnp
from jax import lax
from jax.experimental import pallas as pl
from jax.experimental.pallas import tpu as pltpu

D_MODEL = 1024
DEPTH = 4
N_POOL_GROUPS = 4
POOL_WIDTH = 256
HEAD_DIM = 64
N_Q_HEADS = 8
N_KV_HEADS = 2
Q_WIDTH = 512
KV_WIDTH = 128
WINDOW = 128
ROT_DIM = 16
ROPE_THETA = 500000.0
CONV_WIDTH = 256
D_FF = 2816
ALPHA = (2 * DEPTH) ** 0.25
LN_EPS = 1e-5
MASK_VALUE = -1e30

COL_POOL = 0
COL_Q = COL_POOL + POOL_WIDTH
COL_K = COL_Q + Q_WIDTH
COL_V = COL_K + KV_WIDTH
COL_XC = COL_V + KV_WIDTH
COL_GATES = COL_XC + 3 * CONV_WIDTH
IN_WIDTH = COL_GATES + 3 * D_MODEL

LANES = 128
SUBLANES = 8
MIXER_TILE = 512
FFN_TILE = 512
MIXER_TAIL_BLOCKS = 2
FFN_TAIL_BLOCKS = 4
ROPE_TILE = 2048
POOL_HALO = 16
CONV_HALO = SUBLANES
FFN_CHUNK = 512
MERGE_CHUNK = 512
GATE_BLOCK = 256
HEAD_GATE_BLOCKS = 4
VMEM_LIMIT_BYTES = 58 * 1024 * 1024

BF16 = jnp.bfloat16
F32 = jnp.float32


def _dot(a, b):
    return jnp.dot(a, b, preferred_element_type=F32)


def _dot_nt(a, b):
    return lax.dot_general(a, b, (((1,), (1,)), ((), ())), preferred_element_type=F32)


def _layer_norm(y, g, b):
    mu = jnp.mean(y, axis=-1, keepdims=True)
    yc = y - mu
    var = jnp.mean(yc * yc, axis=-1, keepdims=True)
    return yc * lax.rsqrt(var + LN_EPS) * g + b


def rope_table_kernel(pos_ref, invf_ref, cos_ref, sin_ref):
    T = ROPE_TILE
    ang = pos_ref[0].astype(F32) * invf_ref[...]
    c8 = jnp.cos(ang)
    s8 = jnp.sin(ang)
    ones = jnp.ones((HEAD_DIM - ROT_DIM, T), F32)
    zeros = jnp.zeros((HEAD_DIM - ROT_DIM, T), F32)
    cos_rows = jnp.concatenate([c8, c8, ones, c8, c8, ones], axis=0)
    sin_rows = jnp.concatenate([-s8, s8, zeros, -s8, s8, zeros], axis=0)
    cos_ref[0] = cos_rows.T
    sin_ref[0] = sin_rows.T


def mixer_kernel(x_ref, cos_ref, sin_ref, bias_ref, sinks_ref, win_ref, wpool_ref,
                 pscale_ref, convw_ref, wa_ref, wb_ref, wc_ref, wo_ref, g_ref, b_ref,
                 o_ref, ubuf, zbuf, kvbuf, qbuf, obuf, mbuf, gbuf, *, T, tail_blocks):
    nblk = T // WINDOW
    i = pl.program_id(1)

    @pl.when(i == 0)
    def _():
        ubuf[0:POOL_HALO, :] = jnp.zeros((POOL_HALO, POOL_WIDTH), F32)
        zbuf[0:CONV_HALO, :] = jnp.zeros((CONV_HALO, CONV_WIDTH), F32)
        kvbuf[:, 0:WINDOW, :] = jnp.zeros((8, WINDOW, LANES), BF16)

    xf = x_ref[0]
    xb = xf.astype(BF16)

    lane = lax.broadcasted_iota(jnp.int32, (T, LANES), 1)
    low = lane < HEAD_DIM

    n_units = nblk * N_KV_HEADS
    gate_blocks = [(COL_GATES + c, c) for c in range(0, 3 * D_MODEL, GATE_BLOCK)]
    rest = gate_blocks[HEAD_GATE_BLOCKS:]
    unit_blocks = [rest[(u * len(rest)) // n_units:((u + 1) * len(rest)) // n_units] for u in range(n_units)]

    def gate_pieces(blocks):
        for src, dst in blocks:
            gbuf[:, dst:dst + GATE_BLOCK] = jnp.tanh(_dot(xb, win_ref[:, src:src + GATE_BLOCK]))

    H = POOL_HALO
    ubuf[H:H + T, :] = _dot(xb, win_ref[:, COL_POOL:COL_POOL + POOL_WIDTH])
    cc = _dot(xb, win_ref[:, COL_XC:COL_XC + 3 * CONV_WIDTH])
    qkv = _dot(xb, win_ref[:, COL_Q:COL_XC])
    gate_pieces(gate_blocks[:HEAD_GATE_BLOCKS])

    u0 = ubuf[H:H + T, 0:LANES]
    u1 = ubuf[H:H + T, LANES:2 * LANES]
    s2 = u0 + ubuf[H - 1:H - 1 + T, 0:LANES]
    s4 = s2 + ubuf[H - 2:H - 2 + T, 0:LANES] + ubuf[H - 3:H - 3 + T, 0:LANES]
    e8 = ubuf[H - 8:H + T, LANES:2 * LANES]
    for j in range(1, 8):
        e8 = e8 + ubuf[H - 8 - j:H + T - j, LANES:2 * LANES]
    s8 = e8[8:, :]
    s16 = s8 + e8[:T, :]
    tpos1 = lax.broadcasted_iota(jnp.int32, (T, LANES), 0) + (i * T + 1)
    cnt_a = jnp.where(low, jnp.minimum(tpos1, 2), jnp.minimum(tpos1, 4)).astype(F32)
    cnt_b = jnp.where(low, jnp.minimum(tpos1, 8), jnp.minimum(tpos1, 16)).astype(F32)
    pooled_a = jnp.where(low, s2, s4) / cnt_a - u0
    pooled_b = jnp.where(low, s8, s16) / cnt_b - u1
    pooled = jnp.concatenate([pooled_a, pooled_b], axis=1).astype(BF16)
    o_a = (_dot(pooled, wpool_ref[...]) * pscale_ref[...]).astype(BF16)
    ubuf[0:H, :] = ubuf[T:T + H, :]

    Z = CONV_HALO
    xc = cc[:, 0:CONV_WIDTH]
    gate_b = cc[:, CONV_WIDTH:2 * CONV_WIDTH]
    gate_c = cc[:, 2 * CONV_WIDTH:3 * CONV_WIDTH]
    z = gate_c * xc
    zbuf[Z:Z + T, :] = z
    cw = convw_ref[...]
    conv = zbuf[Z - 2:Z - 2 + T, :] * cw[0:1, :] + zbuf[Z - 1:Z - 1 + T, :] * cw[1:2, :] + z * cw[2:3, :]
    o_c = (gate_b * conv).astype(BF16)
    zbuf[0:Z, :] = zbuf[T:T + Z, :]

    cosv = cos_ref[0]
    sinv = sin_ref[0]
    rot_low = (lane & (HEAD_DIM - 1)) < (ROT_DIM // 2)

    def rope(t):
        partner = jnp.where(rot_low, pltpu.roll(t, LANES - ROT_DIM // 2, axis=1),
                            pltpu.roll(t, ROT_DIM // 2, axis=1))
        return t * cosv + partner * sinv

    for m in range(N_Q_HEADS // 2):
        qm = (rope(qkv[:, m * LANES:(m + 1) * LANES]) * (HEAD_DIM ** -0.5)).astype(BF16)
        for n in range(nblk):
            qbuf[m // 2, n, (m % 2) * WINDOW:(m % 2 + 1) * WINDOW, :] = qm[n * WINDOW:(n + 1) * WINDOW, :]

    kr = rope(qkv[:, Q_WIDTH:Q_WIDTH + KV_WIDTH])
    vv = qkv[:, Q_WIDTH + KV_WIDTH:Q_WIDTH + 2 * KV_WIDTH]
    zero = jnp.zeros((T, LANES), F32)
    for base, t in ((0, kr), (4, vv)):
        tr = pltpu.roll(t, HEAD_DIM, axis=1)
        kvbuf[base + 0, WINDOW:WINDOW + T, :] = jnp.where(low, t, zero).astype(BF16)
        kvbuf[base + 1, WINDOW:WINDOW + T, :] = jnp.where(low, zero, tr).astype(BF16)
        kvbuf[base + 2, WINDOW:WINDOW + T, :] = jnp.where(low, tr, zero).astype(BF16)
        kvbuf[base + 3, WINDOW:WINDOW + T, :] = jnp.where(low, zero, t).astype(BF16)

    y_side = {}

    def side_piece(br, o_br, w_ref, c0):
        def run():
            y_side[(br, c0)] = _dot(o_br, w_ref[:, c0:c0 + MERGE_CHUNK])
        return run

    side_list = [side_piece(br, o_br, w_ref, c0) for br, o_br, w_ref in ((0, o_a, wa_ref), (2, o_c, wc_ref))
                 for c0 in range(0, D_MODEL, MERGE_CHUNK)]
    side_units = {n_units - len(side_list) + k: f for k, f in enumerate(side_list)}

    low_w = lax.broadcasted_iota(jnp.int32, (WINDOW, LANES), 1) < HEAD_DIM
    first = jnp.where(i == 0, 1, 0)
    for n in range(nblk):
        bias = bias_ref[first] if n == 0 else bias_ref[0]
        seg = pl.ds(n * WINDOW, 2 * WINDOW)
        for h in range(N_KV_HEADS):
            qs = qbuf[h, n]
            s_lo = _dot_nt(qs, kvbuf[2 * h, seg, :])
            s_hi = _dot_nt(qs, kvbuf[2 * h + 1, seg, :])
            unit = n * N_KV_HEADS + h
            gate_pieces(unit_blocks[unit])
            if unit in side_units:
                side_units[unit]()
            es = {}
            rs = {}
            for half, s_all in ((0, s_lo), (1, s_hi)):
                for j in range(2):
                    head = 4 * h + 2 * j + half
                    s = s_all[j * WINDOW:(j + 1) * WINDOW, :] + bias
                    sink = sinks_ref[head]
                    mx = jnp.maximum(jnp.max(s, axis=-1, keepdims=True), sink)
                    e = jnp.exp(s - mx)
                    den = jnp.sum(e, axis=-1, keepdims=True) + jnp.exp(sink - mx)
                    es[(half, j)] = e.astype(BF16)
                    rs[(half, j)] = 1.0 / den
            p_lo = jnp.concatenate([es[(0, 0)], es[(0, 1)]], axis=0)
            p_hi = jnp.concatenate([es[(1, 0)], es[(1, 1)]], axis=0)
            o = _dot(p_lo, kvbuf[4 + 2 * h, seg, :]) + _dot(p_hi, kvbuf[4 + 2 * h + 1, seg, :])
            for j in range(2):
                oj = o[j * WINDOW:(j + 1) * WINDOW, :]
                col = (2 * h + j) * LANES
                obuf[n * WINDOW:(n + 1) * WINDOW, col:col + LANES] = jnp.where(
                    low_w, oj * rs[(0, j)], oj * rs[(1, j)]).astype(BF16)
    kvbuf[:, 0:WINDOW, :] = kvbuf[:, T:T + WINDOW, :]
    o_b = obuf[...]

    for c0 in range(0, D_MODEL, MERGE_CHUNK):
        cs = slice(c0, c0 + MERGE_CHUNK)
        acc = None
        for br, (o_br, w_ref) in enumerate(((o_a, wa_ref), (o_b, wb_ref), (o_c, wc_ref))):
            g0 = br * D_MODEL + c0
            y_br = y_side[(br, c0)] if (br, c0) in y_side else _dot(o_br, w_ref[:, cs])
            term = gbuf[:, g0:g0 + MERGE_CHUNK] * y_br + y_br
            acc = term if acc is None else acc + term
        mbuf[:, cs] = acc.astype(BF16)
    rows = T // tail_blocks
    for r0 in range(0, T, rows):
        mix = _dot(mbuf[r0:r0 + rows, :], wo_ref[...])
        o_ref[0, r0:r0 + rows, :] = _layer_norm(ALPHA * xf[r0:r0 + rows, :] + mix, g_ref[...], b_ref[...])


def ffn_kernel(x_ref, wup_ref, cw_ref, wdown_ref, g_ref, b_ref, o_ref, carry, hbuf, *, T, tail_blocks):
    Z = CONV_HALO
    i = pl.program_id(1)

    @pl.when(i == 0)
    def _():
        carry[...] = jnp.zeros(carry.shape, F32)

    xf = x_ref[0]
    xb = xf.astype(BF16)

    def conv3(up, prev, cw):
        body = pltpu.roll(up, 2, axis=0) * cw[0:1, :] + pltpu.roll(up, 1, axis=0) * cw[1:2, :] + up * cw[2:3, :]
        lead = jnp.concatenate([prev, up[0:2 * Z, :]], axis=0)
        fix = (pltpu.roll(lead, 2, axis=0) * cw[0:1, :] + pltpu.roll(lead, 1, axis=0) * cw[1:2, :]
               + lead * cw[2:3, :])[Z:, :]
        return body, fix

    for c0 in range(0, D_FF, FFN_CHUNK):
        C = min(FFN_CHUNK, D_FF - c0)
        bodies, fixes = [], []
        for part in range(2):
            cols = slice(part * D_FF + c0, part * D_FF + c0 + C)
            up = _dot(xb, wup_ref[:, cols])
            cw = cw_ref[:, cols] * (0.5 if part == 0 else 1.0)
            body, fix = conv3(up, carry[:, cols], cw)
            carry[:, cols] = up[T - Z:, :]
            bodies.append(body)
            fixes.append(fix)
        act = lambda half_a, b: ((half_a + half_a * jnp.tanh(half_a)) * b).astype(BF16)
        hbuf[:, c0:c0 + C] = act(*bodies)
        hbuf[0:2 * Z, c0:c0 + C] = act(*fixes)
    rows = T // tail_blocks
    for r0 in range(0, T, rows):
        ffn = _dot(hbuf[r0:r0 + rows, :], wdown_ref[...])
        o_ref[0, r0:r0 + rows, :] = _layer_norm(ALPHA * xf[r0:r0 + rows, :] + ffn, g_ref[...], b_ref[...])


def _resident(shape, layer):
    nd = len(shape)
    return pl.BlockSpec((None,) + tuple(shape), lambda b, i: (layer,) + (0,) * nd,
                        pipeline_mode=pl.Buffered(1))


def _compiler_params():
    return pltpu.CompilerParams(dimension_semantics=("arbitrary", "arbitrary"),
                                vmem_limit_bytes=VMEM_LIMIT_BYTES)


def _rope_call(batch, seq):
    T = ROPE_TILE
    out = pl.BlockSpec((1, T, LANES), lambda b, i: (b, i, 0))
    return pl.pallas_call(
        rope_table_kernel,
        grid=(batch, seq // T),
        in_specs=[pl.BlockSpec((1, 1, T), lambda b, i: (b, 0, i)),
                  pl.BlockSpec((ROT_DIM // 2, T), lambda b, i: (0, 0))],
        out_specs=[out, out],
        out_shape=[jax.ShapeDtypeStruct((batch, seq, LANES), F32)] * 2,
        name="rope_table",
    )


def _mixer_call(layer, batch, seq, T, tail_blocks):
    tok = lambda width: pl.BlockSpec((1, T, width), lambda b, i: (b, i, 0))
    in_specs = [
        tok(D_MODEL), tok(LANES), tok(LANES),
        pl.BlockSpec((2, WINDOW, 2 * WINDOW), lambda b, i: (0, 0, 0)),
        pl.BlockSpec(memory_space=pltpu.SMEM),
        _resident((D_MODEL, IN_WIDTH), layer),
        _resident((POOL_WIDTH, POOL_WIDTH), layer),
        _resident((1, POOL_WIDTH), layer),
        _resident((3, CONV_WIDTH), layer),
        _resident((POOL_WIDTH, D_MODEL), layer),
        _resident((Q_WIDTH, D_MODEL), layer),
        _resident((CONV_WIDTH, D_MODEL), layer),
        _resident((D_MODEL, D_MODEL), layer),
        _resident((1, D_MODEL), layer),
        _resident((1, D_MODEL), layer),
    ]
    scratch = [
        pltpu.VMEM((POOL_HALO + T, POOL_WIDTH), F32),
        pltpu.VMEM((CONV_HALO + T, CONV_WIDTH), F32),
        pltpu.VMEM((8, WINDOW + T, LANES), BF16),
        pltpu.VMEM((N_KV_HEADS, T // WINDOW, 2 * WINDOW, LANES), BF16),
        pltpu.VMEM((T, Q_WIDTH), BF16),
        pltpu.VMEM((T, D_MODEL), BF16),
        pltpu.VMEM((T, 3 * D_MODEL), F32),
    ]
    return pl.pallas_call(
        functools.partial(mixer_kernel, T=T, tail_blocks=tail_blocks),
        grid=(batch, seq // T),
        in_specs=in_specs,
        out_specs=tok(D_MODEL),
        out_shape=jax.ShapeDtypeStruct((batch, seq, D_MODEL), F32),
        scratch_shapes=scratch,
        compiler_params=_compiler_params(),
        name=f"mixer_l{layer}",
    )


def _ffn_call(layer, batch, seq, T, tail_blocks):
    tok = pl.BlockSpec((1, T, D_MODEL), lambda b, i: (b, i, 0))
    in_specs = [
        tok,
        _resident((D_MODEL, 2 * D_FF), layer),
        _resident((3, 2 * D_FF), layer),
        _resident((D_FF, D_MODEL), layer),
        _resident((1, D_MODEL), layer),
        _resident((1, D_MODEL), layer),
    ]
    scratch = [
        pltpu.VMEM((CONV_HALO, 2 * D_FF), F32),
        pltpu.VMEM((T, D_FF), BF16),
    ]
    return pl.pallas_call(
        functools.partial(ffn_kernel, T=T, tail_blocks=tail_blocks),
        grid=(batch, seq // T),
        in_specs=in_specs,
        out_specs=tok,
        out_shape=jax.ShapeDtypeStruct((batch, seq, D_MODEL), F32),
        scratch_shapes=scratch,
        compiler_params=_compiler_params(),
        name=f"ffn_l{layer}",
    )


def _band_bias():
    qi = jnp.arange(WINDOW)[:, None]
    kj = jnp.arange(2 * WINDOW)[None, :]
    band = (kj > qi) & (kj <= qi + WINDOW)
    start = band & (kj >= WINDOW)
    as_bias = lambda valid: jnp.where(valid, 0.0, MASK_VALUE).astype(F32)
    return jnp.stack([as_bias(band), as_bias(start)])


def kernel(x, positions, w_in, w_pool, pool_scale, attn_sinks, conv_w, w_branch_a, w_branch_b,
           w_branch_c, w_o, ln1_g, ln1_b, w_up, ffn_conv_w, w_down, ln2_g, ln2_b):
    batch, seq, _ = x.shape
    assert seq % MIXER_TILE == 0 and seq % FFN_TILE == 0 and seq % ROPE_TILE == 0 and MIXER_TILE % WINDOW == 0
    inv_freq = ROPE_THETA ** (-jnp.arange(0, ROT_DIM, 2, dtype=F32) / ROT_DIM)
    inv_freq_rows = jnp.broadcast_to(inv_freq[:, None], (ROT_DIM // 2, ROPE_TILE))
    cos_t, sin_t = _rope_call(batch, seq)(positions.reshape(batch, 1, seq), inv_freq_rows)
    bias = _band_bias()

    eye = jnp.eye(N_POOL_GROUPS, dtype=w_pool.dtype)
    wpool_bd = jnp.einsum('lgcd,gh->lgchd', w_pool, eye).reshape(DEPTH, POOL_WIDTH, POOL_WIDTH)

    gate_half = jnp.concatenate([jnp.ones((COL_GATES,), F32), jnp.full((3 * D_MODEL,), 0.5, F32)])
    row = lambda v: v.reshape(DEPTH, 1, v.shape[-1])
    w_in_b = (w_in * gate_half).astype(BF16)
    wpool_b = wpool_bd.astype(BF16)
    wa_b, wb_b, wc_b = w_branch_a.astype(BF16), w_branch_b.astype(BF16), w_branch_c.astype(BF16)
    wo_b = (w_o * 0.5).astype(BF16)
    wup_b = w_up.astype(BF16)
    wdown_b = w_down.astype(BF16)

    for l in range(DEPTH):
        x = _mixer_call(l, batch, seq, MIXER_TILE, MIXER_TAIL_BLOCKS)(
            x, cos_t, sin_t, bias, attn_sinks[l], w_in_b, wpool_b, row(pool_scale), conv_w,
            wa_b, wb_b, wc_b, wo_b, row(ln1_g), row(ln1_b))
        x = _ffn_call(l, batch, seq, FFN_TILE, FFN_TAIL_BLOCKS)(
            x, wup_b, ffn_conv_w, wdown_b, row(ln2_g), row(ln2_b))
    return x
```

```python
import functools

import jax
import jax.numpy as jnp
from jax import lax
from jax.experimental import pallas as pl
from jax.experimental.pallas import tpu as pltpu

D_MODEL = 1024
DEPTH = 4
N_POOL_GROUPS = 4
POOL_WIDTH = 256
HEAD_DIM = 64
N_Q_HEADS = 8
N_KV_HEADS = 2
Q_WIDTH = 512
KV_WIDTH = 128
WINDOW = 128
ROT_DIM = 16
ROPE_THETA = 500000.0
CONV_WIDTH = 256
D_FF = 2816
ALPHA = (2 * DEPTH) ** 0.25
LN_EPS = 1e-5
MASK_VALUE = -1e30

COL_POOL = 0
COL_Q = COL_POOL + POOL_WIDTH
COL_K = COL_Q + Q_WIDTH
COL_V = COL_K + KV_WIDTH
COL_XC = COL_V + KV_WIDTH
COL_GATES = COL_XC + 3 * CONV_WIDTH
IN_WIDTH = COL_GATES + 3 * D_MODEL

LANES = 128
SUBLANES = 8
MIXER_TILE = 512
FFN_TILE = 512
MIXER_TAIL_BLOCKS = 2
FFN_TAIL_BLOCKS = 2
ROPE_TILE = 2048
POOL_HALO = 16
CONV_HALO = SUBLANES
FFN_CHUNK = 512
MERGE_CHUNK = 512
GATE_BLOCK = 256
HEAD_GATE_BLOCKS = 4
VMEM_LIMIT_BYTES = 58 * 1024 * 1024

BF16 = jnp.bfloat16
F32 = jnp.float32


def _dot(a, b):
    return jnp.dot(a, b, preferred_element_type=F32)


def _dot_nt(a, b):
    return lax.dot_general(a, b, (((1,), (1,)), ((), ())), preferred_element_type=F32)


def _layer_norm(y, g, b):
    mu = jnp.mean(y, axis=-1, keepdims=True)
    yc = y - mu
    var = jnp.mean(yc * yc, axis=-1, keepdims=True)
    return yc * lax.rsqrt(var + LN_EPS) * g + b


def rope_table_kernel(pos_ref, invf_ref, cos_ref, sin_ref):
    T = ROPE_TILE
    ang = pos_ref[0].astype(F32) * invf_ref[...]
    c8 = jnp.cos(ang)
    s8 = jnp.sin(ang)
    ones = jnp.ones((HEAD_DIM - ROT_DIM, T), F32)
    zeros = jnp.zeros((HEAD_DIM - ROT_DIM, T), F32)
    cos_rows = jnp.concatenate([c8, c8, ones, c8, c8, ones], axis=0)
    sin_rows = jnp.concatenate([-s8, s8, zeros, -s8, s8, zeros], axis=0)
    cos_ref[0] = cos_rows.T
    sin_ref[0] = sin_rows.T


def mixer_kernel(x_ref, cos_ref, sin_ref, bias_ref, sinks_ref, win_ref, wpool_ref,
                 pscale_ref, convw_ref, wa_ref, wb_ref, wc_ref, wo_ref, g_ref, b_ref,
                 o_ref, ubuf, zbuf, kvbuf, qbuf, obuf, mbuf, gbuf, *, T, tail_blocks):
    nblk = T // WINDOW
    i = pl.program_id(1)

    @pl.when(i == 0)
    def _():
        ubuf[0:POOL_HALO, :] = jnp.zeros((POOL_HALO, POOL_WIDTH), F32)
        zbuf[0:CONV_HALO, :] = jnp.zeros((CONV_HALO, CONV_WIDTH), F32)
        kvbuf[:, 0:WINDOW, :] = jnp.zeros((8, WINDOW, LANES), BF16)

    xf = x_ref[0]
    xb = xf.astype(BF16)

    lane = lax.broadcasted_iota(jnp.int32, (T, LANES), 1)
    low = lane < HEAD_DIM

    n_units = nblk * N_KV_HEADS
    gate_blocks = [(COL_GATES + c, c) for c in range(0, 3 * D_MODEL, GATE_BLOCK)]
    rest = gate_blocks[HEAD_GATE_BLOCKS:]
    unit_blocks = [rest[(u * len(rest)) // n_units:((u + 1) * len(rest)) // n_units] for u in range(n_units)]

    def gate_pieces(blocks):
        for src, dst in blocks:
            gbuf[:, dst:dst + GATE_BLOCK] = jnp.tanh(_dot(xb, win_ref[:, src:src + GATE_BLOCK]))

    H = POOL_HALO
    ubuf[H:H + T, :] = _dot(xb, win_ref[:, COL_POOL:COL_POOL + POOL_WIDTH])
    cc = _dot(xb, win_ref[:, COL_XC:COL_XC + 3 * CONV_WIDTH])
    qkv = _dot(xb, win_ref[:, COL_Q:COL_XC])
    gate_pieces(gate_blocks[:HEAD_GATE_BLOCKS])

    u0 = ubuf[H:H + T, 0:LANES]
    u1 = ubuf[H:H + T, LANES:2 * LANES]
    s2 = u0 + ubuf[H - 1:H - 1 + T, 0:LANES]
    s4 = s2 + ubuf[H - 2:H - 2 + T, 0:LANES] + ubuf[H - 3:H - 3 + T, 0:LANES]
    e8 = ubuf[H - 8:H + T, LANES:2 * LANES]
    for j in range(1, 8):
        e8 = e8 + ubuf[H - 8 - j:H + T - j, LANES:2 * LANES]
    s8 = e8[8:, :]
    s16 = s8 + e8[:T, :]
    tpos1 = lax.broadcasted_iota(jnp.int32, (T, LANES), 0) + (i * T + 1)
    cnt_a = jnp.where(low, jnp.minimum(tpos1, 2), jnp.minimum(tpos1, 4)).astype(F32)
    cnt_b = jnp.where(low, jnp.minimum(tpos1, 8), jnp.minimum(tpos1, 16)).astype(F32)
    pooled_a = jnp.where(low, s2, s4) / cnt_a - u0
    pooled_b = jnp.where(low, s8, s16) / cnt_b - u1
    pooled = jnp.concatenate([pooled_a, pooled_b], axis=1).astype(BF16)
    o_a = (_dot(pooled, wpool_ref[...]) * pscale_ref[...]).astype(BF16)
    ubuf[0:H, :] = ubuf[T:T + H, :]

    Z = CONV_HALO
    xc = cc[:, 0:CONV_WIDTH]
    gate_b = cc[:, CONV_WIDTH:2 * CONV_WIDTH]
    gate_c = cc[:, 2 * CONV_WIDTH:3 * CONV_WIDTH]
    z = gate_c * xc
    zbuf[Z:Z + T, :] = z
    cw = convw_ref[...]
    conv = zbuf[Z - 2:Z - 2 + T, :] * cw[0:1, :] + zbuf[Z - 1:Z - 1 + T, :] * cw[1:2, :] + z * cw[2:3, :]
    o_c = (gate_b * conv).astype(BF16)
    zbuf[0:Z, :] = zbuf[T:T + Z, :]

    cosv = cos_ref[0]
    sinv = sin_ref[0]
    rot_low = (lane & (HEAD_DIM - 1)) < (ROT_DIM // 2)

    def rope(t):
        partner = jnp.where(rot_low, pltpu.roll(t, LANES - ROT_DIM // 2, axis=1),
                            pltpu.roll(t, ROT_DIM // 2, axis=1))
        return t * cosv + partner * sinv

    for m in range(N_Q_HEADS // 2):
        qm = (rope(qkv[:, m * LANES:(m + 1) * LANES]) * (HEAD_DIM ** -0.5)).astype(BF16)
        for n in range(nblk):
            qbuf[m // 2, n, (m % 2) * WINDOW:(m % 2 + 1) * WINDOW, :] = qm[n * WINDOW:(n + 1) * WINDOW, :]

    kr = rope(qkv[:, Q_WIDTH:Q_WIDTH + KV_WIDTH])
    vv = qkv[:, Q_WIDTH + KV_WIDTH:Q_WIDTH + 2 * KV_WIDTH]
    zero = jnp.zeros((T, LANES), F32)
    for base, t in ((0, kr), (4, vv)):
        tr = pltpu.roll(t, HEAD_DIM, axis=1)
        kvbuf[base + 0, WINDOW:WINDOW + T, :] = jnp.where(low, t, zero).astype(BF16)
        kvbuf[base + 1, WINDOW:WINDOW + T, :] = jnp.where(low, zero, tr).astype(BF16)
        kvbuf[base + 2, WINDOW:WINDOW + T, :] = jnp.where(low, tr, zero).astype(BF16)
        kvbuf[base + 3, WINDOW:WINDOW + T, :] = jnp.where(low, zero, t).astype(BF16)

    y_side = {}

    def side_piece(br, o_br, w_ref, c0):
        def run():
            y_side[(br, c0)] = _dot(o_br, w_ref[:, c0:c0 + MERGE_CHUNK])
        return run

    side_list = [side_piece(br, o_br, w_ref, c0) for br, o_br, w_ref in ((0, o_a, wa_ref), (2, o_c, wc_ref))
                 for c0 in range(0, D_MODEL, MERGE_CHUNK)]
    side_units = {n_units - len(side_list) + k: f for k, f in enumerate(side_list)}

    low_w = lax.broadcasted_iota(jnp.int32, (WINDOW, LANES), 1) < HEAD_DIM
    first = jnp.where(i == 0, 1, 0)
    for n in range(nblk):
        bias = bias_ref[first] if n == 0 else bias_ref[0]
        seg = pl.ds(n * WINDOW, 2 * WINDOW)
        for h in range(N_KV_HEADS):
            qs = qbuf[h, n]
            s_lo = _dot_nt(qs, kvbuf[2 * h, seg, :])
            s_hi = _dot_nt(qs, kvbuf[2 * h + 1, seg, :])
            unit = n * N_KV_HEADS + h
            gate_pieces(unit_blocks[unit])
            if unit in side_units:
                side_units[unit]()
            es = {}
            rs = {}
            for half, s_all in ((0, s_lo), (1, s_hi)):
                for j in range(2):
                    head = 4 * h + 2 * j + half
                    s = s_all[j * WINDOW:(j + 1) * WINDOW, :] + bias
                    sink = sinks_ref[head]
                    mx = jnp.maximum(jnp.max(s, axis=-1, keepdims=True), sink)
                    e = jnp.exp(s - mx)
                    den = jnp.sum(e, axis=-1, keepdims=True) + jnp.exp(sink - mx)
                    es[(half, j)] = e.astype(BF16)
                    rs[(half, j)] = 1.0 / den
            p_lo = jnp.concatenate([es[(0, 0)], es[(0, 1)]], axis=0)
            p_hi = jnp.concatenate([es[(1, 0)], es[(1, 1)]], axis=0)
            o = _dot(p_lo, kvbuf[4 + 2 * h, seg, :]) + _dot(p_hi, kvbuf[4 + 2 * h + 1, seg, :])
            for j in range(2):
                oj = o[j * WINDOW:(j + 1) * WINDOW, :]
                col = (2 * h + j) * LANES
                obuf[n * WINDOW:(n + 1) * WINDOW, col:col + LANES] = jnp.where(
                    low_w, oj * rs[(0, j)], oj * rs[(1, j)]).astype(BF16)
    kvbuf[:, 0:WINDOW, :] = kvbuf[:, T:T + WINDOW, :]
    o_b = obuf[...]

    for c0 in range(0, D_MODEL, MERGE_CHUNK):
        cs = slice(c0, c0 + MERGE_CHUNK)
        acc = None
        for br, (o_br, w_ref) in enumerate(((o_a, wa_ref), (o_b, wb_ref), (o_c, wc_ref))):
            g0 = br * D_MODEL + c0
            y_br = y_side[(br, c0)] if (br, c0) in y_side else _dot(o_br, w_ref[:, cs])
            term = gbuf[:, g0:g0 + MERGE_CHUNK] * y_br + y_br
            acc = term if acc is None else acc + term
        mbuf[:, cs] = acc.astype(BF16)
    rows = T // tail_blocks
    for r0 in range(0, T, rows):
        mix = _dot(mbuf[r0:r0 + rows, :], wo_ref[...])
        o_ref[0, r0:r0 + rows, :] = _layer_norm(ALPHA * xf[r0:r0 + rows, :] + mix, g_ref[...], b_ref[...])


def ffn_kernel(x_ref, wup_ref, cw_ref, wdown_ref, g_ref, b_ref, o_ref, carry, hbuf, *, T, tail_blocks):
    Z = CONV_HALO
    i = pl.program_id(1)

    @pl.when(i == 0)
    def _():
        carry[...] = jnp.zeros(carry.shape, F32)

    xf = x_ref[0]
    xb = xf.astype(BF16)

    def conv3(up, prev, cw):
        body = pltpu.roll(up, 2, axis=0) * cw[0:1, :] + pltpu.roll(up, 1, axis=0) * cw[1:2, :] + up * cw[2:3, :]
        lead = jnp.concatenate([prev, up[0:2 * Z, :]], axis=0)
        fix = (pltpu.roll(lead, 2, axis=0) * cw[0:1, :] + pltpu.roll(lead, 1, axis=0) * cw[1:2, :]
               + lead * cw[2:3, :])[Z:, :]
        return body, fix

    for c0 in range(0, D_FF, FFN_CHUNK):
        C = min(FFN_CHUNK, D_FF - c0)
        bodies, fixes = [], []
        for part in range(2):
            cols = slice(part * D_FF + c0, part * D_FF + c0 + C)
            up = _dot(xb, wup_ref[:, cols])
            cw = cw_ref[:, cols] * (0.5 if part == 0 else 1.0)
            body, fix = conv3(up, carry[:, cols], cw)
            carry[:, cols] = up[T - Z:, :]
            bodies.append(body)
            fixes.append(fix)
        act = lambda half_a, b: ((half_a + half_a * jnp.tanh(half_a)) * b).astype(BF16)
        hbuf[:, c0:c0 + C] = act(*bodies)
        hbuf[0:2 * Z, c0:c0 + C] = act(*fixes)
    rows = T // tail_blocks
    for r0 in range(0, T, rows):
        ffn = _dot(hbuf[r0:r0 + rows, :], wdown_ref[...])
        o_ref[0, r0:r0 + rows, :] = _layer_norm(ALPHA * xf[r0:r0 + rows, :] + ffn, g_ref[...], b_ref[...])


def _resident(shape, layer):
    nd = len(shape)
    return pl.BlockSpec((None,) + tuple(shape), lambda b, i: (layer,) + (0,) * nd,
                        pipeline_mode=pl.Buffered(1))


def _compiler_params():
    return pltpu.CompilerParams(dimension_semantics=("arbitrary", "arbitrary"),
                                vmem_limit_bytes=VMEM_LIMIT_BYTES)


def _rope_call(batch, seq):
    T = ROPE_TILE
    out = pl.BlockSpec((1, T, LANES), lambda b, i: (b, i, 0))
    return pl.pallas_call(
        rope_table_kernel,
        grid=(batch, seq // T),
        in_specs=[pl.BlockSpec((1, 1, T), lambda b, i: (b, 0, i)),
                  pl.BlockSpec((ROT_DIM // 2, T), lambda b, i: (0, 0))],
        out_specs=[out, out],
        out_shape=[jax.ShapeDtypeStruct((batch, seq, LANES), F32)] * 2,
        name="rope_table",
    )


def _mixer_call(layer, batch, seq, T, tail_blocks):
    tok = lambda width: pl.BlockSpec((1, T, width), lambda b, i: (b, i, 0))
    in_specs = [
        tok(D_MODEL), tok(LANES), tok(LANES),
        pl.BlockSpec((2, WINDOW, 2 * WINDOW), lambda b, i: (0, 0, 0)),
        pl.BlockSpec(memory_space=pltpu.SMEM),
        _resident((D_MODEL, IN_WIDTH), layer),
        _resident((POOL_WIDTH, POOL_WIDTH), layer),
        _resident((1, POOL_WIDTH), layer),
        _resident((3, CONV_WIDTH), layer),
        _resident((POOL_WIDTH, D_MODEL), layer),
        _resident((Q_WIDTH, D_MODEL), layer),
        _resident((CONV_WIDTH, D_MODEL), layer),
        _resident((D_MODEL, D_MODEL), layer),
        _resident((1, D_MODEL), layer),
        _resident((1, D_MODEL), layer),
    ]
    scratch = [
        pltpu.VMEM((POOL_HALO + T, POOL_WIDTH), F32),
        pltpu.VMEM((CONV_HALO + T, CONV_WIDTH), F32),
        pltpu.VMEM((8, WINDOW + T, LANES), BF16),
        pltpu.VMEM((N_KV_HEADS, T // WINDOW, 2 * WINDOW, LANES), BF16),
        pltpu.VMEM((T, Q_WIDTH), BF16),
        pltpu.VMEM((T, D_MODEL), BF16),
        pltpu.VMEM((T, 3 * D_MODEL), F32),
    ]
    return pl.pallas_call(
        functools.partial(mixer_kernel, T=T, tail_blocks=tail_blocks),
        grid=(batch, seq // T),
        in_specs=in_specs,
        out_specs=tok(D_MODEL),
        out_shape=jax.ShapeDtypeStruct((batch, seq, D_MODEL), F32),
        scratch_shapes=scratch,
        compiler_params=_compiler_params(),
        name=f"mixer_l{layer}",
    )


def _ffn_call(layer, batch, seq, T, tail_blocks):
    tok = pl.BlockSpec((1, T, D_MODEL), lambda b, i: (b, i, 0))
    in_specs = [
        tok,
        _resident((D_MODEL, 2 * D_FF), layer),
        _resident((3, 2 * D_FF), layer),
        _resident((D_FF, D_MODEL), layer),
        _resident((1, D_MODEL), layer),
        _resident((1, D_MODEL), layer),
    ]
    scratch = [
        pltpu.VMEM((CONV_HALO, 2 * D_FF), F32),
        pltpu.VMEM((T, D_FF), BF16),
    ]
    return pl.pallas_call(
        functools.partial(ffn_kernel, T=T, tail_blocks=tail_blocks),
        grid=(batch, seq // T),
        in_specs=in_specs,
        out_specs=tok,
        out_shape=jax.ShapeDtypeStruct((batch, seq, D_MODEL), F32),
        scratch_shapes=scratch,
        compiler_params=_compiler_params(),
        name=f"ffn_l{layer}",
    )


def _band_bias():
    qi = jnp.arange(WINDOW)[:, None]
    kj = jnp.arange(2 * WINDOW)[None, :]
    band = (kj > qi) & (kj <= qi + WINDOW)
    start = band & (kj >= WINDOW)
    as_bias = lambda valid: jnp.where(valid, 0.0, MASK_VALUE).astype(F32)
    return jnp.stack([as_bias(band), as_bias(start)])


def kernel(x, positions, w_in, w_pool, pool_scale, attn_sinks, conv_w, w_branch_a, w_branch_b,
           w_branch_c, w_o, ln1_g, ln1_b, w_up, ffn_conv_w, w_down, ln2_g, ln2_b):
    batch, seq, _ = x.shape
    assert seq % MIXER_TILE == 0 and seq % FFN_TILE == 0 and seq % ROPE_TILE == 0 and MIXER_TILE % WINDOW == 0
    inv_freq = ROPE_THETA ** (-jnp.arange(0, ROT_DIM, 2, dtype=F32) / ROT_DIM)
    inv_freq_rows = jnp.broadcast_to(inv_freq[:, None], (ROT_DIM // 2, ROPE_TILE))
    cos_t, sin_t = _rope_call(batch, seq)(positions.reshape(batch, 1, seq), inv_freq_rows)
    bias = _band_bias()

    eye = jnp.eye(N_POOL_GROUPS, dtype=w_pool.dtype)
    wpool_bd = jnp.einsum('lgcd,gh->lgchd', w_pool, eye).reshape(DEPTH, POOL_WIDTH, POOL_WIDTH)

    gate_half = jnp.concatenate([jnp.ones((COL_GATES,), F32), jnp.full((3 * D_MODEL,), 0.5, F32)])
    row = lambda v: v.reshape(DEPTH, 1, v.shape[-1])
    w_in_b = (w_in * gate_half).astype(BF16)
    wpool_b = wpool_bd.astype(BF16)
    wa_b, wb_b, wc_b = w_branch_a.astype(BF16), w_branch_b.astype(BF16), w_branch_c.astype(BF16)
    wo_b = (w_o * 0.5).astype(BF16)
    wup_b = w_up.astype(BF16)
    wdown_b = w_down.astype(BF16)

    for l in range(DEPTH):
        x = _mixer_call(l, batch, seq, MIXER_TILE, MIXER_TAIL_BLOCKS)(
            x, cos_t, sin_t, bias, attn_sinks[l], w_in_b, wpool_b, row(pool_scale), conv_w,
            wa_b, wb_b, wc_b, wo_b, row(ln1_g), row(ln1_b))
        x = _ffn_call(l, batch, seq, FFN_TILE, FFN_TAIL_BLOCKS)(
            x, wup_b, ffn_conv_w, wdown_b, row(ln2_g), row(ln2_b))
    return x
```

```python
import functools

import jax
import jax.numpy as jnp
from jax import lax
from jax.experimental import pallas as pl
from jax.experimental.pallas import tpu as pltpu

D_MODEL = 1024
DEPTH = 4
N_POOL_GROUPS = 4
POOL_WIDTH = 256
HEAD_DIM = 64
N_Q_HEADS = 8
N_KV_HEADS = 2
Q_WIDTH = 512
KV_WIDTH = 128
WINDOW = 128
ROT_DIM = 16
ROPE_THETA = 500000.0
CONV_WIDTH = 256
D_FF = 2816
ALPHA = (2 * DEPTH) ** 0.25
LN_EPS = 1e-5
MASK_VALUE = -1e30

COL_POOL = 0
COL_Q = COL_POOL + POOL_WIDTH
COL_K = COL_Q + Q_WIDTH
COL_V = COL_K + KV_WIDTH
COL_XC = COL_V + KV_WIDTH
COL_GATES = COL_XC + 3 * CONV_WIDTH
IN_WIDTH = COL_GATES + 3 * D_MODEL

LANES = 128
SUBLANES = 8
MIXER_TILE = 512
FFN_TILE = 512
MIXER_TAIL_BLOCKS = 2
FFN_TAIL_BLOCKS = 2
ROPE_TILE = 2048
POOL_HALO = 16
CONV_HALO = SUBLANES
FFN_CHUNK_VARIANTS = (512, 768, 1024, 768)
FFN_CHUNK = 512
MERGE_CHUNK = 512
GATE_BLOCK = 256
HEAD_GATE_BLOCKS = 4
VMEM_LIMIT_BYTES = 58 * 1024 * 1024

BF16 = jnp.bfloat16
F32 = jnp.float32


def _dot(a, b):
    return jnp.dot(a, b, preferred_element_type=F32)


def _dot_nt(a, b):
    return lax.dot_general(a, b, (((1,), (1,)), ((), ())), preferred_element_type=F32)


def _layer_norm(y, g, b):
    mu = jnp.mean(y, axis=-1, keepdims=True)
    yc = y - mu
    var = jnp.mean(yc * yc, axis=-1, keepdims=True)
    return yc * lax.rsqrt(var + LN_EPS) * g + b


def rope_table_kernel(pos_ref, invf_ref, cos_ref, sin_ref):
    T = ROPE_TILE
    ang = pos_ref[0].astype(F32) * invf_ref[...]
    c8 = jnp.cos(ang)
    s8 = jnp.sin(ang)
    ones = jnp.ones((HEAD_DIM - ROT_DIM, T), F32)
    zeros = jnp.zeros((HEAD_DIM - ROT_DIM, T), F32)
    cos_rows = jnp.concatenate([c8, c8, ones, c8, c8, ones], axis=0)
    sin_rows = jnp.concatenate([-s8, s8, zeros, -s8, s8, zeros], axis=0)
    cos_ref[0] = cos_rows.T
    sin_ref[0] = sin_rows.T


def mixer_kernel(x_ref, cos_ref, sin_ref, bias_ref, sinks_ref, win_ref, wpool_ref,
                 pscale_ref, convw_ref, wa_ref, wb_ref, wc_ref, wo_ref, g_ref, b_ref,
                 o_ref, ubuf, zbuf, kvbuf, qbuf, obuf, mbuf, gbuf, *, T, tail_blocks):
    nblk = T // WINDOW
    i = pl.program_id(1)

    @pl.when(i == 0)
    def _():
        ubuf[0:POOL_HALO, :] = jnp.zeros((POOL_HALO, POOL_WIDTH), F32)
        zbuf[0:CONV_HALO, :] = jnp.zeros((CONV_HALO, CONV_WIDTH), F32)
        kvbuf[:, 0:WINDOW, :] = jnp.zeros((8, WINDOW, LANES), BF16)

    xf = x_ref[0]
    xb = xf.astype(BF16)

    lane = lax.broadcasted_iota(jnp.int32, (T, LANES), 1)
    low = lane < HEAD_DIM

    n_units = nblk * N_KV_HEADS
    gate_blocks = [(COL_GATES + c, c) for c in range(0, 3 * D_MODEL, GATE_BLOCK)]
    rest = gate_blocks[HEAD_GATE_BLOCKS:]
    unit_blocks = [rest[(u * len(rest)) // n_units:((u + 1) * len(rest)) // n_units] for u in range(n_units)]

    def gate_pieces(blocks):
        for src, dst in blocks:
            gbuf[:, dst:dst + GATE_BLOCK] = jnp.tanh(_dot(xb, win_ref[:, src:src + GATE_BLOCK]))

    H = POOL_HALO
    ubuf[H:H + T, :] = _dot(xb, win_ref[:, COL_POOL:COL_POOL + POOL_WIDTH])
    cc = _dot(xb, win_ref[:, COL_XC:COL_XC + 3 * CONV_WIDTH])
    qkv = _dot(xb, win_ref[:, COL_Q:COL_XC])
    gate_pieces(gate_blocks[:HEAD_GATE_BLOCKS])

    u0 = ubuf[H:H + T, 0:LANES]
    u1 = ubuf[H:H + T, LANES:2 * LANES]
    s2 = u0 + ubuf[H - 1:H - 1 + T, 0:LANES]
    s4 = s2 + ubuf[H - 2:H - 2 + T, 0:LANES] + ubuf[H - 3:H - 3 + T, 0:LANES]
    e8 = ubuf[H - 8:H + T, LANES:2 * LANES]
    for j in range(1, 8):
        e8 = e8 + ubuf[H - 8 - j:H + T - j, LANES:2 * LANES]
    s8 = e8[8:, :]
    s16 = s8 + e8[:T, :]
    tpos1 = lax.broadcasted_iota(jnp.int32, (T, LANES), 0) + (i * T + 1)
    cnt_a = jnp.where(low, jnp.minimum(tpos1, 2), jnp.minimum(tpos1, 4)).astype(F32)
    cnt_b = jnp.where(low, jnp.minimum(tpos1, 8), jnp.minimum(tpos1, 16)).astype(F32)
    pooled_a = jnp.where(low, s2, s4) / cnt_a - u0
    pooled_b = jnp.where(low, s8, s16) / cnt_b - u1
    pooled = jnp.concatenate([pooled_a, pooled_b], axis=1).astype(BF16)
    o_a = (_dot(pooled, wpool_ref[...]) * pscale_ref[...]).astype(BF16)
    ubuf[0:H, :] = ubuf[T:T + H, :]

    Z = CONV_HALO
    xc = cc[:, 0:CONV_WIDTH]
    gate_b = cc[:, CONV_WIDTH:2 * CONV_WIDTH]
    gate_c = cc[:, 2 * CONV_WIDTH:3 * CONV_WIDTH]
    z = gate_c * xc
    zbuf[Z:Z + T, :] = z
    cw = convw_ref[...]
    conv = zbuf[Z - 2:Z - 2 + T, :] * cw[0:1, :] + zbuf[Z - 1:Z - 1 + T, :] * cw[1:2, :] + z * cw[2:3, :]
    o_c = (gate_b * conv).astype(BF16)
    zbuf[0:Z, :] = zbuf[T:T + Z, :]

    cosv = cos_ref[0]
    sinv = sin_ref[0]
    rot_low = (lane & (HEAD_DIM - 1)) < (ROT_DIM // 2)

    def rope(t):
        partner = jnp.where(rot_low, pltpu.roll(t, LANES - ROT_DIM // 2, axis=1),
                            pltpu.roll(t, ROT_DIM // 2, axis=1))
        return t * cosv + partner * sinv

    for m in range(N_Q_HEADS // 2):
        qm = (rope(qkv[:, m * LANES:(m + 1) * LANES]) * (HEAD_DIM ** -0.5)).astype(BF16)
        for n in range(nblk):
            qbuf[m // 2, n, (m % 2) * WINDOW:(m % 2 + 1) * WINDOW, :] = qm[n * WINDOW:(n + 1) * WINDOW, :]

    kr = rope(qkv[:, Q_WIDTH:Q_WIDTH + KV_WIDTH])
    vv = qkv[:, Q_WIDTH + KV_WIDTH:Q_WIDTH + 2 * KV_WIDTH]
    zero = jnp.zeros((T, LANES), F32)
    for base, t in ((0, kr), (4, vv)):
        tr = pltpu.roll(t, HEAD_DIM, axis=1)
        kvbuf[base + 0, WINDOW:WINDOW + T, :] = jnp.where(low, t, zero).astype(BF16)
        kvbuf[base + 1, WINDOW:WINDOW + T, :] = jnp.where(low, zero, tr).astype(BF16)
        kvbuf[base + 2, WINDOW:WINDOW + T, :] = jnp.where(low, tr, zero).astype(BF16)
        kvbuf[base + 3, WINDOW:WINDOW + T, :] = jnp.where(low, zero, t).astype(BF16)

    y_side = {}

    def side_piece(br, o_br, w_ref, c0):
        def run():
            y_side[(br, c0)] = _dot(o_br, w_ref[:, c0:c0 + MERGE_CHUNK])
        return run

    side_list = [side_piece(br, o_br, w_ref, c0) for br, o_br, w_ref in ((0, o_a, wa_ref), (2, o_c, wc_ref))
                 for c0 in range(0, D_MODEL, MERGE_CHUNK)]
    side_units = {n_units - len(side_list) + k: f for k, f in enumerate(side_list)}

    low_w = lax.broadcasted_iota(jnp.int32, (WINDOW, LANES), 1) < HEAD_DIM
    first = jnp.where(i == 0, 1, 0)
    for n in range(nblk):
        bias = bias_ref[first] if n == 0 else bias_ref[0]
        seg = pl.ds(n * WINDOW, 2 * WINDOW)
        for h in range(N_KV_HEADS):
            qs = qbuf[h, n]
            s_lo = _dot_nt(qs, kvbuf[2 * h, seg, :])
            s_hi = _dot_nt(qs, kvbuf[2 * h + 1, seg, :])
            unit = n * N_KV_HEADS + h
            gate_pieces(unit_blocks[unit])
            if unit in side_units:
                side_units[unit]()
            es = {}
            rs = {}
            for half, s_all in ((0, s_lo), (1, s_hi)):
                for j in range(2):
                    head = 4 * h + 2 * j + half
                    s = s_all[j * WINDOW:(j + 1) * WINDOW, :] + bias
                    sink = sinks_ref[head]
                    mx = jnp.maximum(jnp.max(s, axis=-1, keepdims=True), sink)
                    e = jnp.exp(s - mx)
                    den = jnp.sum(e, axis=-1, keepdims=True) + jnp.exp(sink - mx)
                    es[(half, j)] = e.astype(BF16)
                    rs[(half, j)] = 1.0 / den
            p_lo = jnp.concatenate([es[(0, 0)], es[(0, 1)]], axis=0)
            p_hi = jnp.concatenate([es[(1, 0)], es[(1, 1)]], axis=0)
            o = _dot(p_lo, kvbuf[4 + 2 * h, seg, :]) + _dot(p_hi, kvbuf[4 + 2 * h + 1, seg, :])
            for j in range(2):
                oj = o[j * WINDOW:(j + 1) * WINDOW, :]
                col = (2 * h + j) * LANES
                obuf[n * WINDOW:(n + 1) * WINDOW, col:col + LANES] = jnp.where(
                    low_w, oj * rs[(0, j)], oj * rs[(1, j)]).astype(BF16)
    kvbuf[:, 0:WINDOW, :] = kvbuf[:, T:T + WINDOW, :]
    o_b = obuf[...]

    for c0 in range(0, D_MODEL, MERGE_CHUNK):
        cs = slice(c0, c0 + MERGE_CHUNK)
        acc = None
        for br, (o_br, w_ref) in enumerate(((o_a, wa_ref), (o_b, wb_ref), (o_c, wc_ref))):
            g0 = br * D_MODEL + c0
            y_br = y_side[(br, c0)] if (br, c0) in y_side else _dot(o_br, w_ref[:, cs])
            term = gbuf[:, g0:g0 + MERGE_CHUNK] * y_br + y_br
            acc = term if acc is None else acc + term
        mbuf[:, cs] = acc.astype(BF16)
    rows = T // tail_blocks
    for r0 in range(0, T, rows):
        mix = _dot(mbuf[r0:r0 + rows, :], wo_ref[...])
        o_ref[0, r0:r0 + rows, :] = _layer_norm(ALPHA * xf[r0:r0 + rows, :] + mix, g_ref[...], b_ref[...])


def ffn_kernel(x_ref, wup_ref, cw_ref, wdown_ref, g_ref, b_ref, o_ref, carry, hbuf, *, T, tail_blocks, chunk):
    Z = CONV_HALO
    i = pl.program_id(1)

    @pl.when(i == 0)
    def _():
        carry[...] = jnp.zeros(carry.shape, F32)

    xf = x_ref[0]
    xb = xf.astype(BF16)

    def conv3(up, prev, cw):
        body = pltpu.roll(up, 2, axis=0) * cw[0:1, :] + pltpu.roll(up, 1, axis=0) * cw[1:2, :] + up * cw[2:3, :]
        lead = jnp.concatenate([prev, up[0:2 * Z, :]], axis=0)
        fix = (pltpu.roll(lead, 2, axis=0) * cw[0:1, :] + pltpu.roll(lead, 1, axis=0) * cw[1:2, :]
               + lead * cw[2:3, :])[Z:, :]
        return body, fix

    for c0 in range(0, D_FF, chunk):
        C = min(chunk, D_FF - c0)
        bodies, fixes = [], []
        for part in range(2):
            cols = slice(part * D_FF + c0, part * D_FF + c0 + C)
            up = _dot(xb, wup_ref[:, cols])
            cw = cw_ref[:, cols] * (0.5 if part == 0 else 1.0)
            body, fix = conv3(up, carry[:, cols], cw)
            carry[:, cols] = up[T - Z:, :]
            bodies.append(body)
            fixes.append(fix)
        act = lambda half_a, b: ((half_a + half_a * jnp.tanh(half_a)) * b).astype(BF16)
        hbuf[:, c0:c0 + C] = act(*bodies)
        hbuf[0:2 * Z, c0:c0 + C] = act(*fixes)
    rows = T // tail_blocks
    for r0 in range(0, T, rows):
        ffn = _dot(hbuf[r0:r0 + rows, :], wdown_ref[...])
        o_ref[0, r0:r0 + rows, :] = _layer_norm(ALPHA * xf[r0:r0 + rows, :] + ffn, g_ref[...], b_ref[...])


def _resident(shape, layer):
    nd = len(shape)
    return pl.BlockSpec((None,) + tuple(shape), lambda b, i: (layer,) + (0,) * nd,
                        pipeline_mode=pl.Buffered(1))


def _compiler_params():
    return pltpu.CompilerParams(dimension_semantics=("arbitrary", "arbitrary"),
                                vmem_limit_bytes=VMEM_LIMIT_BYTES)


def _rope_call(batch, seq):
    T = ROPE_TILE
    out = pl.BlockSpec((1, T, LANES), lambda b, i: (b, i, 0))
    return pl.pallas_call(
        rope_table_kernel,
        grid=(batch, seq // T),
        in_specs=[pl.BlockSpec((1, 1, T), lambda b, i: (b, 0, i)),
                  pl.BlockSpec((ROT_DIM // 2, T), lambda b, i: (0, 0))],
        out_specs=[out, out],
        out_shape=[jax.ShapeDtypeStruct((batch, seq, LANES), F32)] * 2,
        name="rope_table",
    )


def _mixer_call(layer, batch, seq, T, tail_blocks):
    tok = lambda width: pl.BlockSpec((1, T, width), lambda b, i: (b, i, 0))
    in_specs = [
        tok(D_MODEL), tok(LANES), tok(LANES),
        pl.BlockSpec((2, WINDOW, 2 * WINDOW), lambda b, i: (0, 0, 0)),
        pl.BlockSpec(memory_space=pltpu.SMEM),
        _resident((D_MODEL, IN_WIDTH), layer),
        _resident((POOL_WIDTH, POOL_WIDTH), layer),
        _resident((1, POOL_WIDTH), layer),
        _resident((3, CONV_WIDTH), layer),
        _resident((POOL_WIDTH, D_MODEL), layer),
        _resident((Q_WIDTH, D_MODEL), layer),
        _resident((CONV_WIDTH, D_MODEL), layer),
        _resident((D_MODEL, D_MODEL), layer),
        _resident((1, D_MODEL), layer),
        _resident((1, D_MODEL), layer),
    ]
    scratch = [
        pltpu.VMEM((POOL_HALO + T, POOL_WIDTH), F32),
        pltpu.VMEM((CONV_HALO + T, CONV_WIDTH), F32),
        pltpu.VMEM((8, WINDOW + T, LANES), BF16),
        pltpu.VMEM((N_KV_HEADS, T // WINDOW, 2 * WINDOW, LANES), BF16),
        pltpu.VMEM((T, Q_WIDTH), BF16),
        pltpu.VMEM((T, D_MODEL), BF16),
        pltpu.VMEM((T, 3 * D_MODEL), F32),
    ]
    return pl.pallas_call(
        functools.partial(mixer_kernel, T=T, tail_blocks=tail_blocks),
        grid=(batch, seq // T),
        in_specs=in_specs,
        out_specs=tok(D_MODEL),
        out_shape=jax.ShapeDtypeStruct((batch, seq, D_MODEL), F32),
        scratch_shapes=scratch,
        compiler_params=_compiler_params(),
        name=f"mixer_l{layer}",
    )


def _ffn_call(layer, batch, seq, T, tail_blocks, chunk):
    tok = pl.BlockSpec((1, T, D_MODEL), lambda b, i: (b, i, 0))
    in_specs = [
        tok,
        _resident((D_MODEL, 2 * D_FF), layer),
        _resident((3, 2 * D_FF), layer),
        _resident((D_FF, D_MODEL), layer),
        _resident((1, D_MODEL), layer),
        _resident((1, D_MODEL), layer),
    ]
    scratch = [
        pltpu.VMEM((CONV_HALO, 2 * D_FF), F32),
        pltpu.VMEM((T, D_FF), BF16),
    ]
    return pl.pallas_call(
        functools.partial(ffn_kernel, T=T, tail_blocks=tail_blocks, chunk=chunk),
        grid=(batch, seq // T),
        in_specs=in_specs,
        out_specs=tok,
        out_shape=jax.ShapeDtypeStruct((batch, seq, D_MODEL), F32),
        scratch_shapes=scratch,
        compiler_params=_compiler_params(),
        name=f"ffn_l{layer}",
    )


def _band_bias():
    qi = jnp.arange(WINDOW)[:, None]
    kj = jnp.arange(2 * WINDOW)[None, :]
    band = (kj > qi) & (kj <= qi + WINDOW)
    start = band & (kj >= WINDOW)
    as_bias = lambda valid: jnp.where(valid, 0.0, MASK_VALUE).astype(F32)
    return jnp.stack([as_bias(band), as_bias(start)])


def kernel(x, positions, w_in, w_pool, pool_scale, attn_sinks, conv_w, w_branch_a, w_branch_b,
           w_branch_c, w_o, ln1_g, ln1_b, w_up, ffn_conv_w, w_down, ln2_g, ln2_b):
    batch, seq, _ = x.shape
    assert seq % MIXER_TILE == 0 and seq % FFN_TILE == 0 and seq % ROPE_TILE == 0 and MIXER_TILE % WINDOW == 0
    inv_freq = ROPE_THETA ** (-jnp.arange(0, ROT_DIM, 2, dtype=F32) / ROT_DIM)
    inv_freq_rows = jnp.broadcast_to(inv_freq[:, None], (ROT_DIM // 2, ROPE_TILE))
    cos_t, sin_t = _rope_call(batch, seq)(positions.reshape(batch, 1, seq), inv_freq_rows)
    bias = _band_bias()

    eye = jnp.eye(N_POOL_GROUPS, dtype=w_pool.dtype)
    wpool_bd = jnp.einsum('lgcd,gh->lgchd', w_pool, eye).reshape(DEPTH, POOL_WIDTH, POOL_WIDTH)

    gate_half = jnp.concatenate([jnp.ones((COL_GATES,), F32), jnp.full((3 * D_MODEL,), 0.5, F32)])
    row = lambda v: v.reshape(DEPTH, 1, v.shape[-1])
    w_in_b = (w_in * gate_half).astype(BF16)
    wpool_b = wpool_bd.astype(BF16)
    wa_b, wb_b, wc_b = w_branch_a.astype(BF16), w_branch_b.astype(BF16), w_branch_c.astype(BF16)
    wo_b = (w_o * 0.5).astype(BF16)
    wup_b = w_up.astype(BF16)
    wdown_b = w_down.astype(BF16)

    for l in range(DEPTH):
        x = _mixer_call(l, batch, seq, MIXER_TILE, MIXER_TAIL_BLOCKS)(
            x, cos_t, sin_t, bias, attn_sinks[l], w_in_b, wpool_b, row(pool_scale), conv_w,
            wa_b, wb_b, wc_b, wo_b, row(ln1_g), row(ln1_b))
        x = _ffn_call(l, batch, seq, FFN_TILE, FFN_TAIL_BLOCKS, FFN_CHUNK_VARIANTS[l])(
            x, wup_b, ffn_conv_w, wdown_b, row(ln2_g), row(ln2_b))
    return x
```

```python
import functools

import jax
import jax.numpy as jnp
from jax import lax
from jax.experimental import pallas as pl
from jax.experimental.pallas import tpu as pltpu

D_MODEL = 1024
DEPTH = 4
N_POOL_GROUPS = 4
POOL_WIDTH = 256
HEAD_DIM = 64
N_Q_HEADS = 8
N_KV_HEADS = 2
Q_WIDTH = 512
KV_WIDTH = 128
WINDOW = 128
ROT_DIM = 16
ROPE_THETA = 500000.0
CONV_WIDTH = 256
D_FF = 2816
ALPHA = (2 * DEPTH) ** 0.25
LN_EPS = 1e-5
MASK_VALUE = -1e30

COL_POOL = 0
COL_Q = COL_POOL + POOL_WIDTH
COL_K = COL_Q + Q_WIDTH
COL_V = COL_K + KV_WIDTH
COL_XC = COL_V + KV_WIDTH
COL_GATES = COL_XC + 3 * CONV_WIDTH
IN_WIDTH = COL_GATES + 3 * D_MODEL

LANES = 128
SUBLANES = 8
MIXER_TILE = 512
FFN_TILE = 512
MIXER_TAIL_BLOCKS = 2
FFN_TAIL_BLOCKS = 2
ROPE_TILE = 2048
POOL_HALO = 16
CONV_HALO = SUBLANES
FFN_CHUNK = 768
MERGE_CHUNK = 512
GATE_BLOCK = 256
HEAD_GATE_BLOCKS = 4
VMEM_LIMIT_BYTES = 58 * 1024 * 1024

BF16 = jnp.bfloat16
F32 = jnp.float32


def _dot(a, b):
    return jnp.dot(a, b, preferred_element_type=F32)


def _dot_nt(a, b):
    return lax.dot_general(a, b, (((1,), (1,)), ((), ())), preferred_element_type=F32)


def _layer_norm(y, g, b):
    mu = jnp.mean(y, axis=-1, keepdims=True)
    yc = y - mu
    var = jnp.mean(yc * yc, axis=-1, keepdims=True)
    return yc * lax.rsqrt(var + LN_EPS) * g + b


def rope_table_kernel(pos_ref, invf_ref, cos_ref, sin_ref):
    T = ROPE_TILE
    ang = pos_ref[0].astype(F32) * invf_ref[...]
    c8 = jnp.cos(ang)
    s8 = jnp.sin(ang)
    ones = jnp.ones((HEAD_DIM - ROT_DIM, T), F32)
    zeros = jnp.zeros((HEAD_DIM - ROT_DIM, T), F32)
    cos_rows = jnp.concatenate([c8, c8, ones, c8, c8, ones], axis=0)
    sin_rows = jnp.concatenate([-s8, s8, zeros, -s8, s8, zeros], axis=0)
    cos_ref[0] = cos_rows.T
    sin_ref[0] = sin_rows.T


def mixer_kernel(x_ref, cos_ref, sin_ref, bias_ref, sinks_ref, win_ref, wpool_ref,
                 pscale_ref, convw_ref, wa_ref, wb_ref, wc_ref, wo_ref, g_ref, b_ref,
                 o_ref, ubuf, zbuf, kvbuf, qbuf, obuf, mbuf, gbuf, *, T, tail_blocks):
    nblk = T // WINDOW
    i = pl.program_id(1)

    @pl.when(i == 0)
    def _():
        ubuf[0:POOL_HALO, :] = jnp.zeros((POOL_HALO, POOL_WIDTH), F32)
        zbuf[0:CONV_HALO, :] = jnp.zeros((CONV_HALO, CONV_WIDTH), F32)
        kvbuf[:, 0:WINDOW, :] = jnp.zeros((8, WINDOW, LANES), BF16)

    xf = x_ref[0]
    xb = xf.astype(BF16)

    lane = lax.broadcasted_iota(jnp.int32, (T, LANES), 1)
    low = lane < HEAD_DIM

    n_units = nblk * N_KV_HEADS
    gate_blocks = [(COL_GATES + c, c) for c in range(0, 3 * D_MODEL, GATE_BLOCK)]
    rest = gate_blocks[HEAD_GATE_BLOCKS:]
    unit_blocks = [rest[(u * len(rest)) // n_units:((u + 1) * len(rest)) // n_units] for u in range(n_units)]

    def gate_pieces(blocks):
        bits = None
        for src, dst in blocks:
            t = jnp.tanh(_dot(xb, win_ref[:, src:src + GATE_BLOCK]))
            gbuf[:, dst:dst + GATE_BLOCK] = t
            tb = lax.bitcast_convert_type(t, jnp.uint32)
            for r0 in range(0, T, WINDOW):
                piece = tb[r0:r0 + WINDOW, :]
                bits = piece if bits is None else bits | piece
        if bits is None:
            return None
        zero_bits = lax.shift_right_logical(lax.shift_right_logical(bits, jnp.uint32(16)), jnp.uint32(16))
        return lax.bitcast_convert_type(zero_bits, F32)

    H = POOL_HALO
    ubuf[H:H + T, :] = _dot(xb, win_ref[:, COL_POOL:COL_POOL + POOL_WIDTH])
    cc = _dot(xb, win_ref[:, COL_XC:COL_XC + 3 * CONV_WIDTH])
    qkv = _dot(xb, win_ref[:, COL_Q:COL_XC])
    anchors = [None, gate_pieces(gate_blocks[:HEAD_GATE_BLOCKS])]

    u0 = ubuf[H:H + T, 0:LANES]
    u1 = ubuf[H:H + T, LANES:2 * LANES]
    s2 = u0 + ubuf[H - 1:H - 1 + T, 0:LANES]
    s4 = s2 + ubuf[H - 2:H - 2 + T, 0:LANES] + ubuf[H - 3:H - 3 + T, 0:LANES]
    e8 = ubuf[H - 8:H + T, LANES:2 * LANES]
    for j in range(1, 8):
        e8 = e8 + ubuf[H - 8 - j:H + T - j, LANES:2 * LANES]
    s8 = e8[8:, :]
    s16 = s8 + e8[:T, :]
    tpos1 = lax.broadcasted_iota(jnp.int32, (T, LANES), 0) + (i * T + 1)
    cnt_a = jnp.where(low, jnp.minimum(tpos1, 2), jnp.minimum(tpos1, 4)).astype(F32)
    cnt_b = jnp.where(low, jnp.minimum(tpos1, 8), jnp.minimum(tpos1, 16)).astype(F32)
    pooled_a = jnp.where(low, s2, s4) / cnt_a - u0
    pooled_b = jnp.where(low, s8, s16) / cnt_b - u1
    pooled = jnp.concatenate([pooled_a, pooled_b], axis=1).astype(BF16)
    o_a = (_dot(pooled, wpool_ref[...]) * pscale_ref[...]).astype(BF16)
    ubuf[0:H, :] = ubuf[T:T + H, :]

    Z = CONV_HALO
    xc = cc[:, 0:CONV_WIDTH]
    gate_b = cc[:, CONV_WIDTH:2 * CONV_WIDTH]
    gate_c = cc[:, 2 * CONV_WIDTH:3 * CONV_WIDTH]
    z = gate_c * xc
    zbuf[Z:Z + T, :] = z
    cw = convw_ref[...]
    conv = zbuf[Z - 2:Z - 2 + T, :] * cw[0:1, :] + zbuf[Z - 1:Z - 1 + T, :] * cw[1:2, :] + z * cw[2:3, :]
    o_c = (gate_b * conv).astype(BF16)
    zbuf[0:Z, :] = zbuf[T:T + Z, :]

    cosv = cos_ref[0]
    sinv = sin_ref[0]
    rot_low = (lane & (HEAD_DIM - 1)) < (ROT_DIM // 2)

    def rope(t):
        partner = jnp.where(rot_low, pltpu.roll(t, LANES - ROT_DIM // 2, axis=1),
                            pltpu.roll(t, ROT_DIM // 2, axis=1))
        return t * cosv + partner * sinv

    for m in range(N_Q_HEADS // 2):
        qm = (rope(qkv[:, m * LANES:(m + 1) * LANES]) * (HEAD_DIM ** -0.5)).astype(BF16)
        for n in range(nblk):
            qbuf[m // 2, n, (m % 2) * WINDOW:(m % 2 + 1) * WINDOW, :] = qm[n * WINDOW:(n + 1) * WINDOW, :]

    kr = rope(qkv[:, Q_WIDTH:Q_WIDTH + KV_WIDTH])
    vv = qkv[:, Q_WIDTH + KV_WIDTH:Q_WIDTH + 2 * KV_WIDTH]
    zero = jnp.zeros((T, LANES), F32)
    for base, t in ((0, kr), (4, vv)):
        tr = pltpu.roll(t, HEAD_DIM, axis=1)
        kvbuf[base + 0, WINDOW:WINDOW + T, :] = jnp.where(low, t, zero).astype(BF16)
        kvbuf[base + 1, WINDOW:WINDOW + T, :] = jnp.where(low, zero, tr).astype(BF16)
        kvbuf[base + 2, WINDOW:WINDOW + T, :] = jnp.where(low, tr, zero).astype(BF16)
        kvbuf[base + 3, WINDOW:WINDOW + T, :] = jnp.where(low, zero, t).astype(BF16)

    y_side = {}

    def side_piece(br, o_br, w_ref, c0):
        def run():
            y_side[(br, c0)] = _dot(o_br, w_ref[:, c0:c0 + MERGE_CHUNK])
        return run

    side_list = [side_piece(br, o_br, w_ref, c0) for br, o_br, w_ref in ((0, o_a, wa_ref), (2, o_c, wc_ref))
                 for c0 in range(0, D_MODEL, MERGE_CHUNK)]
    side_units = {n_units - len(side_list) + k: f for k, f in enumerate(side_list)}

    low_w = lax.broadcasted_iota(jnp.int32, (WINDOW, LANES), 1) < HEAD_DIM
    first = jnp.where(i == 0, 1, 0)
    for n in range(nblk):
        bias = bias_ref[first] if n == 0 else bias_ref[0]
        seg = pl.ds(n * WINDOW, 2 * WINDOW)
        for h in range(N_KV_HEADS):
            qs = qbuf[h, n]
            s_lo = _dot_nt(qs, kvbuf[2 * h, seg, :])
            s_hi = _dot_nt(qs, kvbuf[2 * h + 1, seg, :])
            unit = n * N_KV_HEADS + h
            anchors.append(gate_pieces(unit_blocks[unit]))
            anchor = anchors.pop(0)
            if unit in side_units:
                side_units[unit]()
            es = {}
            rs = {}
            for half, s_all in ((0, s_lo), (1, s_hi)):
                for j in range(2):
                    head = 4 * h + 2 * j + half
                    s = s_all[j * WINDOW:(j + 1) * WINDOW, :] + bias
                    if anchor is not None and half == 0 and j == 0:
                        s = s + anchor
                    sink = sinks_ref[head]
                    mx = jnp.maximum(jnp.max(s, axis=-1, keepdims=True), sink)
                    e = jnp.exp(s - mx)
                    den = jnp.sum(e, axis=-1, keepdims=True) + jnp.exp(sink - mx)
                    es[(half, j)] = e.astype(BF16)
                    rs[(half, j)] = 1.0 / den
            p_lo = jnp.concatenate([es[(0, 0)], es[(0, 1)]], axis=0)
            p_hi = jnp.concatenate([es[(1, 0)], es[(1, 1)]], axis=0)
            o = _dot(p_lo, kvbuf[4 + 2 * h, seg, :]) + _dot(p_hi, kvbuf[4 + 2 * h + 1, seg, :])
            for j in range(2):
                oj = o[j * WINDOW:(j + 1) * WINDOW, :]
                col = (2 * h + j) * LANES
                obuf[n * WINDOW:(n + 1) * WINDOW, col:col + LANES] = jnp.where(
                    low_w, oj * rs[(0, j)], oj * rs[(1, j)]).astype(BF16)
    kvbuf[:, 0:WINDOW, :] = kvbuf[:, T:T + WINDOW, :]
    o_b = obuf[...]

    for c0 in range(0, D_MODEL, MERGE_CHUNK):
        cs = slice(c0, c0 + MERGE_CHUNK)
        acc = None
        for br, (o_br, w_ref) in enumerate(((o_a, wa_ref), (o_b, wb_ref), (o_c, wc_ref))):
            g0 = br * D_MODEL + c0
            y_br = y_side[(br, c0)] if (br, c0) in y_side else _dot(o_br, w_ref[:, cs])
            term = gbuf[:, g0:g0 + MERGE_CHUNK] * y_br + y_br
            acc = term if acc is None else acc + term
        mbuf[:, cs] = acc.astype(BF16)
    rows = T // tail_blocks
    for r0 in range(0, T, rows):
        mix = _dot(mbuf[r0:r0 + rows, :], wo_ref[...])
        o_ref[0, r0:r0 + rows, :] = _layer_norm(ALPHA * xf[r0:r0 + rows, :] + mix, g_ref[...], b_ref[...])


def ffn_kernel(x_ref, wup_ref, cw_ref, wdown_ref, g_ref, b_ref, o_ref, carry, hbuf, *, T, tail_blocks):
    Z = CONV_HALO
    i = pl.program_id(1)

    @pl.when(i == 0)
    def _():
        carry[...] = jnp.zeros(carry.shape, F32)

    xf = x_ref[0]
    xb = xf.astype(BF16)

    def conv3(up, prev, cw):
        body = pltpu.roll(up, 2, axis=0) * cw[0:1, :] + pltpu.roll(up, 1, axis=0) * cw[1:2, :] + up * cw[2:3, :]
        lead = jnp.concatenate([prev, up[0:2 * Z, :]], axis=0)
        fix = (pltpu.roll(lead, 2, axis=0) * cw[0:1, :] + pltpu.roll(lead, 1, axis=0) * cw[1:2, :]
               + lead * cw[2:3, :])[Z:, :]
        return body, fix

    for c0 in range(0, D_FF, FFN_CHUNK):
        C = min(FFN_CHUNK, D_FF - c0)
        bodies, fixes = [], []
        for part in range(2):
            cols = slice(part * D_FF + c0, part * D_FF + c0 + C)
            up = _dot(xb, wup_ref[:, cols])
            cw = cw_ref[:, cols] * (0.5 if part == 0 else 1.0)
            body, fix = conv3(up, carry[:, cols], cw)
            carry[:, cols] = up[T - Z:, :]
            bodies.append(body)
            fixes.append(fix)
        act = lambda half_a, b: ((half_a + half_a * jnp.tanh(half_a)) * b).astype(BF16)
        hbuf[:, c0:c0 + C] = act(*bodies)
        hbuf[0:2 * Z, c0:c0 + C] = act(*fixes)
    rows = T // tail_blocks
    for r0 in range(0, T, rows):
        ffn = _dot(hbuf[r0:r0 + rows, :], wdown_ref[...])
        o_ref[0, r0:r0 + rows, :] = _layer_norm(ALPHA * xf[r0:r0 + rows, :] + ffn, g_ref[...], b_ref[...])


def _resident(shape, layer):
    nd = len(shape)
    return pl.BlockSpec((None,) + tuple(shape), lambda b, i: (layer,) + (0,) * nd,
                        pipeline_mode=pl.Buffered(1))


def _compiler_params():
    return pltpu.CompilerParams(dimension_semantics=("arbitrary", "arbitrary"),
                                vmem_limit_bytes=VMEM_LIMIT_BYTES)


def _rope_call(batch, seq):
    T = ROPE_TILE
    out = pl.BlockSpec((1, T, LANES), lambda b, i: (b, i, 0))
    return pl.pallas_call(
        rope_table_kernel,
        grid=(batch, seq // T),
        in_specs=[pl.BlockSpec((1, 1, T), lambda b, i: (b, 0, i)),
                  pl.BlockSpec((ROT_DIM // 2, T), lambda b, i: (0, 0))],
        out_specs=[out, out],
        out_shape=[jax.ShapeDtypeStruct((batch, seq, LANES), F32)] * 2,
        name="rope_table",
    )


def _mixer_call(layer, batch, seq, T, tail_blocks):
    tok = lambda width: pl.BlockSpec((1, T, width), lambda b, i: (b, i, 0))
    in_specs = [
        tok(D_MODEL), tok(LANES), tok(LANES),
        pl.BlockSpec((2, WINDOW, 2 * WINDOW), lambda b, i: (0, 0, 0)),
        pl.BlockSpec(memory_space=pltpu.SMEM),
        _resident((D_MODEL, IN_WIDTH), layer),
        _resident((POOL_WIDTH, POOL_WIDTH), layer),
        _resident((1, POOL_WIDTH), layer),
        _resident((3, CONV_WIDTH), layer),
        _resident((POOL_WIDTH, D_MODEL), layer),
        _resident((Q_WIDTH, D_MODEL), layer),
        _resident((CONV_WIDTH, D_MODEL), layer),
        _resident((D_MODEL, D_MODEL), layer),
        _resident((1, D_MODEL), layer),
        _resident((1, D_MODEL), layer),
    ]
    scratch = [
        pltpu.VMEM((POOL_HALO + T, POOL_WIDTH), F32),
        pltpu.VMEM((CONV_HALO + T, CONV_WIDTH), F32),
        pltpu.VMEM((8, WINDOW + T, LANES), BF16),
        pltpu.VMEM((N_KV_HEADS, T // WINDOW, 2 * WINDOW, LANES), BF16),
        pltpu.VMEM((T, Q_WIDTH), BF16),
        pltpu.VMEM((T, D_MODEL), BF16),
        pltpu.VMEM((T, 3 * D_MODEL), F32),
    ]
    return pl.pallas_call(
        functools.partial(mixer_kernel, T=T, tail_blocks=tail_blocks),
        grid=(batch, seq // T),
        in_specs=in_specs,
        out_specs=tok(D_MODEL),
        out_shape=jax.ShapeDtypeStruct((batch, seq, D_MODEL), F32),
        scratch_shapes=scratch,
        compiler_params=_compiler_params(),
        name=f"mixer_l{layer}",
    )


def _ffn_call(layer, batch, seq, T, tail_blocks):
    tok = pl.BlockSpec((1, T, D_MODEL), lambda b, i: (b, i, 0))
    in_specs = [
        tok,
        _resident((D_MODEL, 2 * D_FF), layer),
        _resident((3, 2 * D_FF), layer),
        _resident((D_FF, D_MODEL), layer),
        _resident((1, D_MODEL), layer),
        _resident((1, D_MODEL), layer),
    ]
    scratch = [
        pltpu.VMEM((CONV_HALO, 2 * D_FF), F32),
        pltpu.VMEM((T, D_FF), BF16),
    ]
    return pl.pallas_call(
        functools.partial(ffn_kernel, T=T, tail_blocks=tail_blocks),
        grid=(batch, seq // T),
        in_specs=in_specs,
        out_specs=tok,
        out_shape=jax.ShapeDtypeStruct((batch, seq, D_MODEL), F32),
        scratch_shapes=scratch,
        compiler_params=_compiler_params(),
        name=f"ffn_l{layer}",
    )


def _band_bias():
    qi = jnp.arange(WINDOW)[:, None]
    kj = jnp.arange(2 * WINDOW)[None, :]
    band = (kj > qi) & (kj <= qi + WINDOW)
    start = band & (kj >= WINDOW)
    as_bias = lambda valid: jnp.where(valid, 0.0, MASK_VALUE).astype(F32)
    return jnp.stack([as_bias(band), as_bias(start)])


def kernel(x, positions, w_in, w_pool, pool_scale, attn_sinks, conv_w, w_branch_a, w_branch_b,
           w_branch_c, w_o, ln1_g, ln1_b, w_up, ffn_conv_w, w_down, ln2_g, ln2_b):
    batch, seq, _ = x.shape
    assert seq % MIXER_TILE == 0 and seq % FFN_TILE == 0 and seq % ROPE_TILE == 0 and MIXER_TILE % WINDOW == 0
    inv_freq = ROPE_THETA ** (-jnp.arange(0, ROT_DIM, 2, dtype=F32) / ROT_DIM)
    inv_freq_rows = jnp.broadcast_to(inv_freq[:, None], (ROT_DIM // 2, ROPE_TILE))
    cos_t, sin_t = _rope_call(batch, seq)(positions.reshape(batch, 1, seq), inv_freq_rows)
    bias = _band_bias()

    eye = jnp.eye(N_POOL_GROUPS, dtype=w_pool.dtype)
    wpool_bd = jnp.einsum('lgcd,gh->lgchd', w_pool, eye).reshape(DEPTH, POOL_WIDTH, POOL_WIDTH)

    gate_half = jnp.concatenate([jnp.ones((COL_GATES,), F32), jnp.full((3 * D_MODEL,), 0.5, F32)])
    row = lambda v: v.reshape(DEPTH, 1, v.shape[-1])
    w_in_b = (w_in * gate_half).astype(BF16)
    wpool_b = wpool_bd.astype(BF16)
    wa_b, wb_b, wc_b = w_branch_a.astype(BF16), w_branch_b.astype(BF16), w_branch_c.astype(BF16)
    wo_b = (w_o * 0.5).astype(BF16)
    wup_b = w_up.astype(BF16)
    wdown_b = w_down.astype(BF16)

    for l in range(DEPTH):
        x = _mixer_call(l, batch, seq, MIXER_TILE, MIXER_TAIL_BLOCKS)(
            x, cos_t, sin_t, bias, attn_sinks[l], w_in_b, wpool_b, row(pool_scale), conv_w,
            wa_b, wb_b, wc_b, wo_b, row(ln1_g), row(ln1_b))
        x = _ffn_call(l, batch, seq, FFN_TILE, FFN_TAIL_BLOCKS)(
            x, wup_b, ffn_conv_w, wdown_b, row(ln2_g), row(ln2_b))
    return x
```
